```python
import jax, jax.numpy as jnp
from jax import lax
import numpy as np

D_MODEL = 1024
BATCH = 16
SEQ = 256
DEPTH = 2
DEC_BATCH = 8
DEC_SEQ = 4096
PAST_LEN = 256

GRID_W = 64
N_DIR = 2
D_MIX = D_MODEL
NORM_EPS = 1e-6
D_LRU = D_MIX // 4
LRU_BLOCKS = 4
LRU_BW = D_LRU // LRU_BLOCKS
CONV_W = 4
LRU_C = 8.0
D_RW = 3 * D_MIX // 8
RW_HD = 64
RW_HEADS = D_RW // RW_HD
RW_R_W = 64
RW_R_A = 64
RW_R_G = 128
RW_GN_EPS = 64e-5
D_GLA = D_MIX - D_LRU - D_RW
GLA_HEADS = 4
GLA_DV = D_GLA // GLA_HEADS
GLA_DK = GLA_DV // 2
D_GLA_K = GLA_HEADS * GLA_DK
GLA_R = 16
GLA_TAU = 16.0
GLA_CHUNK = 64
RW_COLS = 3 * D_RW + 2 * RW_R_W + 2 * RW_R_A + RW_R_G
GLA_COLS = 2 * D_GLA_K + 2 * D_GLA + N_DIR * GLA_R
N_IN = 2 * D_LRU + RW_COLS + GLA_COLS
N_EXPERTS = 16
D_EXPERT = 1024
EC_FACTOR = 2

kernel_name = 'hybrid_lru_rwkv7_gla_ec_moe_diffusion_step'


def _split(x, sizes):
    return jnp.split(x, np.cumsum(sizes)[:-1].tolist(), axis=-1)


def _rmsnorm(x, g):
    xf = x.astype(jnp.float32)
    y = xf * lax.rsqrt(jnp.mean(xf * xf, axis=-1, keepdims=True) + NORM_EPS)
    return (y * g.astype(jnp.float32)).astype(x.dtype)


def _bidir_shared(z):
    return jnp.stack([z, jnp.flip(z, 1)], 0)


def _bidir_split(z):
    return jnp.stack([z[:, :, 0], jnp.flip(z[:, :, 1], 1)], 0)


def _merge_dirs(o):
    return o[0] + jnp.flip(o[1], 1)


def _to_colmajor(z):
    B, L, C = z.shape
    rows = L // GRID_W
    return z.reshape(B, rows, GRID_W, C).transpose(0, 2, 1, 3).reshape(B, L, C)


def _from_colmajor(z):
    B, L, C = z.shape
    rows = L // GRID_W
    return z.reshape(B, GRID_W, rows, C).transpose(0, 2, 1, 3).reshape(B, L, C)


def _lin_combine(e1, e2):
    a1, b1 = e1
    a2, b2 = e2
    return a1 * a2, a2 * b1 + b2


def _rglru(xl, yl, p, h0):
    B, L, C = xl.shape
    f32 = jnp.float32
    pad = (CONV_W // 2, CONV_W - 1 - CONV_W // 2)
    xc = lax.conv_general_dilated(xl, p['lru_conv_w'][:, None, :], window_strides=(1,), padding=[pad],
                                  dimension_numbers=('NWC', 'WIO', 'NWC'), feature_group_count=C) + p['lru_conv_b']
    xs = _bidir_shared(xc)
    xb = xs.reshape(N_DIR, B, L, LRU_BLOCKS, LRU_BW)
    r = jax.nn.sigmoid(jnp.einsum('dblni,dnij->dblnj', xb, p['lru_wa']).reshape(N_DIR, B, L, C) + p['lru_ba'][:, None, None, :])
    i = jax.nn.sigmoid(jnp.einsum('dblni,dnij->dblnj', xb, p['lru_wx']).reshape(N_DIR, B, L, C) + p['lru_bx'][:, None, None, :])
    log_a = -LRU_C * r.astype(f32) * jax.nn.softplus(-p['lru_lambda'].astype(f32))[:, None, None, :]
    a = jnp.exp(log_a)
    u = jnp.sqrt(-jnp.expm1(2.0 * log_a)) * (i * xs).astype(f32)
    a_cum, u_cum = lax.associative_scan(_lin_combine, (a, u), axis=2)
    h = a_cum * jnp.moveaxis(h0.astype(f32), 1, 0)[:, :, None, :] + u_cum
    out = _merge_dirs(h) * jax.nn.gelu(yl.astype(f32))
    return out.astype(xl.dtype), jnp.moveaxis(h[:, :, -1], 0, 1)


def _rwkv7(cols, p, S0):
    B, L, _ = cols.shape
    f32 = jnp.float32
    prev = jnp.pad(cols, ((0, 0), (1, 0), (0, 0)))[:, :-1]
    nxt = jnp.pad(cols, ((0, 0), (0, 1), (0, 0)))[:, 1:]
    xs = cols + p['rw_mu'][0] * (prev - cols) + p['rw_mu'][1] * (nxt - cols)
    r, k, v, wd, ad, gd = _split(xs, (D_RW, D_RW, D_RW, N_DIR * RW_R_W, N_DIR * RW_R_A, RW_R_G))
    wd = wd.reshape(B, L, N_DIR, RW_R_W)
    ad = ad.reshape(B, L, N_DIR, RW_R_A)
    w_log = -jax.nn.softplus(-(p['rw_w0'] + jnp.einsum('bldr,drc->bldc', jnp.tanh(wd), p['rw_w2'])).astype(f32)) - 0.5
    decay = jnp.exp(-jnp.exp(w_log))
    a = jax.nn.sigmoid((p['rw_a0'] + jnp.einsum('bldr,drc->bldc', ad, p['rw_a2'])).astype(f32))
    g = jax.nn.sigmoid(gd) @ p['rw_g2']
    heads = lambda z: z.reshape(z.shape[:-1] + (RW_HEADS, RW_HD))
    r, k, v = r.astype(f32), k.astype(f32), v.astype(f32)
    kk = heads(k * p['rw_kk'])
    kk = kk * lax.rsqrt(jnp.sum(kk * kk, axis=-1, keepdims=True) + 1e-12)
    kd = k[:, :, None, :] * (1.0 + (a - 1.0) * p['rw_ka'])
    rh, vh, kdh = heads(r), heads(v), heads(kd)
    seq = (_bidir_shared(rh), _bidir_split(heads(decay)), _bidir_shared(kk),
           _bidir_split(heads(a)), _bidir_shared(vh), _bidir_split(kdh))
    seq = tuple(jnp.moveaxis(z, 2, 0) for z in seq)

    def step(S, inp):
        r_t, w_t, kk_t, a_t, v_t, k_t = inp
        sa = jnp.einsum('dbhij,dbhj->dbhi', S, -kk_t)
        S = S * w_t[..., None, :] + sa[..., None] * (kk_t * a_t)[..., None, :] + v_t[..., None] * k_t[..., None, :]
        return S, jnp.einsum('dbhij,dbhj->dbhi', S, r_t)

    S_fin, o = lax.scan(step, jnp.moveaxis(S0.astype(f32), 1, 0), seq)
    o = _merge_dirs(jnp.moveaxis(o, 0, 2))
    mu = jnp.mean(o, axis=-1, keepdims=True)
    var = jnp.mean(jnp.square(o - mu), axis=-1, keepdims=True)
    on = (o - mu) * lax.rsqrt(var + RW_GN_EPS) * heads(p['rw_ln_w']) + heads(p['rw_ln_b'])
    bonus = jnp.einsum('blhn,bldhn,dhn->blh', rh, kdh, p['rw_rk'])[..., None] * vh
    out = (on + bonus).reshape(B, L, D_RW) * g
    return out.astype(cols.dtype), jnp.moveaxis(S_fin, 0, 1)


def _gla(cols, p, S0):
    B, L, _ = cols.shape
    f32 = jnp.float32
    q, k, v, g, gd = _split(cols, (D_GLA_K, D_GLA_K, D_GLA, D_GLA, N_DIR * GLA_R))
    q = q.astype(f32).reshape(B, L, GLA_HEADS, GLA_DK) * (GLA_DK ** -0.5)
    k = k.astype(f32).reshape(B, L, GLA_HEADS, GLA_DK)
    v = v.astype(f32).reshape(B, L, GLA_HEADS, GLA_DV)
    la = jax.nn.log_sigmoid((jnp.einsum('bldr,drc->bldc', gd.reshape(B, L, N_DIR, GLA_R), p['gla_wg2'])
                             + p['gla_bg']).astype(f32)) / GLA_TAU
    la = la.reshape(B, L, N_DIR, GLA_HEADS, GLA_DK)
    n_chunks = L // GLA_CHUNK
    chunked = lambda z: jnp.moveaxis(z.reshape((N_DIR, B, n_chunks, GLA_CHUNK) + z.shape[3:]), 2, 0)
    seq = (chunked(_bidir_shared(q)), chunked(_bidir_shared(k)), chunked(_bidir_shared(v)), chunked(_bidir_split(la)))
    mask = jnp.tril(jnp.ones((GLA_CHUNK, GLA_CHUNK), dtype=bool))[:, :, None, None]

    def step(S, inp):
        qc, kc, vc, lac = inp
        b = jnp.cumsum(lac, axis=2)
        diff = b[:, :, :, None] - b[:, :, None, :]
        dec = jnp.exp(jnp.where(mask, diff, -jnp.inf))
        att = jnp.einsum('dbthk,dbshk,dbtshk->dbhts', qc, kc, dec)
        o = jnp.einsum('dbhts,dbshv->dbthv', att, vc) + jnp.einsum('dbthk,dbhkv->dbthv', qc * jnp.exp(b), S)
        bl = b[:, :, -1]
        S = S * jnp.exp(bl)[..., None] + jnp.einsum('dbshk,dbshv->dbhkv', kc * jnp.exp(bl[:, :, None] - b), vc)
        return S, o

    S_fin, o = lax.scan(step, jnp.moveaxis(S0.astype(f32), 1, 0), seq)
    o = jnp.moveaxis(o, 0, 2).reshape(N_DIR, B, L, GLA_HEADS, GLA_DV)
    o = _merge_dirs(o)
    o = o * lax.rsqrt(jnp.mean(o * o, axis=-1, keepdims=True) + NORM_EPS) * p['gla_norm_w']
    out = o.reshape(B, L, D_GLA) * jax.nn.silu(g.astype(f32))
    return out.astype(cols.dtype), jnp.moveaxis(S_fin, 0, 1)


def _mixers(h, p, states, grid):
    s_lru, s_rw, s_gla = states
    proj = h @ p['w_in']
    xl, yl, rw_cols, gla_cols = _split(proj, (D_LRU, D_LRU, RW_COLS, GLA_COLS))
    o_lru, n_lru = _rglru(xl, yl, p, s_lru)
    o_rw, n_rw = _rwkv7(rw_cols, p, s_rw)
    if grid:
        gla_cols = _to_colmajor(gla_cols)
    o_gla, n_gla = _gla(gla_cols, p, s_gla)
    if grid:
        o_gla = _from_colmajor(o_gla)
    out = jnp.concatenate([o_lru, o_rw, o_gla], axis=-1) @ p['w_out']
    return out, (n_lru, n_rw, n_gla)


def _expert_choice_ffn(h, p):
    B, N, D = h.shape
    cap = EC_FACTOR * N // N_EXPERTS
    probs = jax.nn.softmax((h @ p['w_router']).astype(jnp.float32), axis=-1)
    gates, idx = lax.top_k(jnp.swapaxes(probs, 1, 2), cap)
    xsel = jax.vmap(lambda hb, ib: hb[ib])(h, idx)
    hid = jax.nn.silu(jnp.einsum('becd,edf->becf', xsel, p['w_gate'])) * jnp.einsum('becd,edf->becf', xsel, p['w_up'])
    yo = jnp.einsum('becf,efd->becd', hid, p['w_down']) * gates[..., None].astype(h.dtype)
    return jax.vmap(lambda ib, yb: jnp.zeros((N, D), yb.dtype).at[ib.reshape(-1)].add(yb.reshape(-1, D)))(idx, yo)


def _layer(x, cvec, p, states, grid):
    mods = (jax.nn.silu(cvec) @ p['w_ada'] + p['b_ada'])[..., None, :]
    sh1, sc1, g1, sh2, sc2, g2 = jnp.split(mods, 6, axis=-1)
    h = _rmsnorm(x, p['norm1_g']) * (1 + sc1) + sh1
    o, new_states = _mixers(h, p, states, grid)
    x = x + g1 * o
    h = _rmsnorm(x, p['norm2_g']) * (1 + sc2) + sh2
    x = x + g2 * _expert_choice_ffn(h, p)
    return x, new_states


def setup_inputs(seed: int = 0) -> dict:
    key = jax.random.key(seed)
    ks = iter(jax.random.split(key, 48))
    f32 = jnp.float32
    nrm = lambda shape, scale: jax.random.normal(next(ks), shape, f32) * scale
    unif = lambda shape, lo, hi: jax.random.uniform(next(ks), shape, f32, lo, hi)
    u = unif((DEPTH, N_DIR, D_LRU), 0.9, 0.999)
    sig = u ** (1.0 / LRU_C)
    return {
        'x_prompt': nrm((BATCH, SEQ, D_MODEL), 1.0),
        'x_sample': nrm((DEC_BATCH, DEC_SEQ, D_MODEL), 1.0),
        'c': nrm((DEC_BATCH, D_MODEL), 1.0),
        'state_lru': nrm((DEC_BATCH, DEPTH, N_DIR, D_LRU), 0.5),
        'state_rwkv': nrm((DEC_BATCH, DEPTH, N_DIR, RW_HEADS, RW_HD, RW_HD), 0.1),
        'state_gla': nrm((DEC_BATCH, DEPTH, N_DIR, GLA_HEADS, GLA_DK, GLA_DV), 0.1),
        'c_ctx': nrm((D_MODEL,), 1.0),
        'norm1_g': 1.0 + nrm((DEPTH, D_MODEL), 0.05),
        'norm2_g': 1.0 + nrm((DEPTH, D_MODEL), 0.05),
        'w_ada': nrm((DEPTH, D_MODEL, 6 * D_MODEL), 0.5 * D_MODEL ** -0.5),
        'b_ada': nrm((DEPTH, 6 * D_MODEL), 0.05),
        'w_in': nrm((DEPTH, D_MODEL, N_IN), D_MODEL ** -0.5),
        'lru_conv_w': nrm((DEPTH, CONV_W, D_LRU), CONV_W ** -0.5),
        'lru_conv_b': nrm((DEPTH, D_LRU), 0.01),
        'lru_wa': nrm((DEPTH, N_DIR, LRU_BLOCKS, LRU_BW, LRU_BW), LRU_BW ** -0.5),
        'lru_ba': nrm((DEPTH, N_DIR, D_LRU), 0.01),
        'lru_wx': nrm((DEPTH, N_DIR, LRU_BLOCKS, LRU_BW, LRU_BW), LRU_BW ** -0.5),
        'lru_bx': nrm((DEPTH, N_DIR, D_LRU), 0.01),
        'lru_lambda': jnp.log(sig) - jnp.log1p(-sig),
        'rw_mu': unif((DEPTH, 2, RW_COLS), 0.0, 0.5),
        'rw_w0': unif((DEPTH, N_DIR, D_RW), -6.0, 1.0),
        'rw_w2': nrm((DEPTH, N_DIR, RW_R_W, D_RW), RW_R_W ** -0.5),
        'rw_a0': nrm((DEPTH, N_DIR, D_RW), 0.1),
        'rw_a2': nrm((DEPTH, N_DIR, RW_R_A, D_RW), RW_R_A ** -0.5),
        'rw_g2': nrm((DEPTH, RW_R_G, D_RW), RW_R_G ** -0.5),
        'rw_kk': 0.85 + nrm((DEPTH, D_RW), 0.05),
        'rw_ka': 1.0 + nrm((DEPTH, D_RW), 0.05),
        'rw_rk': nrm((DEPTH, N_DIR, RW_HEADS, RW_HD), 0.1),
        'rw_ln_w': 1.0 + nrm((DEPTH, D_RW), 0.05),
        'rw_ln_b': nrm((DEPTH, D_RW), 0.01),
        'gla_wg2': nrm((DEPTH, N_DIR, GLA_R, D_GLA_K), GLA_R ** -0.5),
        'gla_bg': nrm((DEPTH, N_DIR, D_GLA_K), 0.5),
        'gla_norm_w': 1.0 + nrm((DEPTH, GLA_HEADS, GLA_DV), 0.05),
        'w_out': nrm((DEPTH, D_MIX, D_MODEL), D_MIX ** -0.5),
        'w_router': nrm((DEPTH, D_MODEL, N_EXPERTS), D_MODEL ** -0.5),
        'w_gate': nrm((DEPTH, N_EXPERTS, D_MODEL, D_EXPERT), D_MODEL ** -0.5),
        'w_up': nrm((DEPTH, N_EXPERTS, D_MODEL, D_EXPERT), D_MODEL ** -0.5),
        'w_down': nrm((DEPTH, N_EXPERTS, D_EXPERT, D_MODEL), D_EXPERT ** -0.5),
        'final_norm_g': 1.0 + nrm((D_MODEL,), 0.05),
    }


def reference(x_prompt, x_sample, c, state_lru, state_rwkv, state_gla, c_ctx, norm1_g, norm2_g, w_ada, b_ada,
              w_in, lru_conv_w, lru_conv_b, lru_wa, lru_ba, lru_wx, lru_bx, lru_lambda, rw_mu, rw_w0, rw_w2,
              rw_a0, rw_a2, rw_g2, rw_kk, rw_ka, rw_rk, rw_ln_w, rw_ln_b, gla_wg2, gla_bg, gla_norm_w, w_out,
              w_router, w_gate, w_up, w_down, final_norm_g):
    def layer_params(l):
        return {'norm1_g': norm1_g[l], 'norm2_g': norm2_g[l], 'w_ada': w_ada[l], 'b_ada': b_ada[l],
                'w_in': w_in[l], 'lru_conv_w': lru_conv_w[l], 'lru_conv_b': lru_conv_b[l],
                'lru_wa': lru_wa[l], 'lru_ba': lru_ba[l], 'lru_wx': lru_wx[l], 'lru_bx': lru_bx[l],
                'lru_lambda': lru_lambda[l], 'rw_mu': rw_mu[l], 'rw_w0': rw_w0[l], 'rw_w2': rw_w2[l],
                'rw_a0': rw_a0[l], 'rw_a2': rw_a2[l], 'rw_g2': rw_g2[l], 'rw_kk': rw_kk[l], 'rw_ka': rw_ka[l],
                'rw_rk': rw_rk[l], 'rw_ln_w': rw_ln_w[l], 'rw_ln_b': rw_ln_b[l], 'gla_wg2': gla_wg2[l],
                'gla_bg': gla_bg[l], 'gla_norm_w': gla_norm_w[l], 'w_out': w_out[l], 'w_router': w_router[l],
                'w_gate': w_gate[l], 'w_up': w_up[l], 'w_down': w_down[l]}

    params = [layer_params(l) for l in range(DEPTH)]
    f32 = jnp.float32

    bp = x_prompt.shape[0]
    zero = (jnp.zeros((bp, N_DIR, D_LRU), f32),
            jnp.zeros((bp, N_DIR, RW_HEADS, RW_HD, RW_HD), f32),
            jnp.zeros((bp, N_DIR, GLA_HEADS, GLA_DK, GLA_DV), f32))
    xp = x_prompt
    ctx_states = []
    for l in range(DEPTH):
        xp, st = _layer(xp, c_ctx, params[l], zero, False)
        ctx_states.append(st)
    y_prompt = _rmsnorm(xp, final_norm_g)
    new_state_lru = jnp.stack([s[0] for s in ctx_states], axis=1).astype(x_prompt.dtype)
    new_state_rwkv = jnp.stack([s[1] for s in ctx_states], axis=1).astype(x_prompt.dtype)
    new_state_gla = jnp.stack([s[2] for s in ctx_states], axis=1).astype(x_prompt.dtype)

    xs = x_sample
    for l in range(DEPTH):
        xs, _ = _layer(xs, c, params[l], (state_lru[:, l], state_rwkv[:, l], state_gla[:, l]), True)
    y_sample = _rmsnorm(xs, final_norm_g)
    return (y_prompt, y_sample, new_state_lru, new_state_rwkv, new_state_gla)
```

```python
import functools

import numpy as np
import jax
import jax.numpy as jnp
from jax import lax
from jax.experimental import pallas as pl
from jax.experimental.pallas import tpu as pltpu

F32 = jnp.float32
BF16 = jnp.bfloat16

D = 1024
DEPTH = 2
NORM_EPS = 1e-6
D_LRU = 256
LRU_BLOCKS = 4
LRU_BW = 64
LRU_C = 8.0
D_RW = 384
RW_H = 6
RW_HD = 64
RW_COLS = 1536
RW_GN_EPS = 64e-5
D_GLA = 384
GLA_H = 4
GLA_DV = 96
GLA_DK = 48
GLA_R = 16
GLA_TAU = 16.0
N_EXPERTS = 16
GRID_W = 64

GLA_KP = 64
GLA_VP = 128
GLA_QW = GLA_H * GLA_KP
GLA_VW = GLA_H * GLA_VP
GLA_PCOLS = 2 * GLA_QW + 2 * GLA_VW + 128

CH = 64
VMEM_LIMIT = 56 * 1024 * 1024


def _cp(sem, vmem=VMEM_LIMIT):
    return pltpu.CompilerParams(dimension_semantics=sem, vmem_limit_bytes=vmem)


def _sigmoid(x):
    return 1.0 / (1.0 + jnp.exp(-x))


def _softplus(x):
    return jnp.maximum(x, 0.0) + jnp.log1p(jnp.exp(-jnp.abs(x)))


def _one_minus_exp(y):
    e = jnp.exp(y)
    em1 = e - 1.0
    small = jnp.where(em1 == 0.0, y, em1 * y / jnp.log(e))
    return -jnp.where(y < -0.5, em1, small)


def _silu(x):
    return x * _sigmoid(x)


def _gelu_tanh(x):
    return 0.5 * x * (1.0 + jnp.tanh(0.7978845608028654 * (x + 0.044715 * (x * x * x))))


_NN = (((1,), (0,)), ((), ()))
_NT = (((1,), (1,)), ((), ()))
_TN = (((0,), (0,)), ((), ()))


def _dg(a, b, dims):
    return lax.dot_general(a, b, dims, preferred_element_type=F32)


def _split2(x):
    hi = x.astype(BF16)
    lo = (x - hi.astype(F32)).astype(BF16)
    return hi, lo


def _split3(x):
    hi = x.astype(BF16)
    r = x - hi.astype(F32)
    mid = r.astype(BF16)
    lo = (r - mid.astype(F32)).astype(BF16)
    return hi, mid, lo


def _mm(a, b, dims=_NN, passes=1):
    if passes == 1:
        return _dg(a.astype(BF16), b.astype(BF16), dims)
    ah, al = _split2(a)
    bh, bl = _split2(b)
    return _dg(ah, bh, dims) + (_dg(ah, bl, dims) + _dg(al, bh, dims))


def _mm_exact_lhs01(m01, x):
    hi, mid, lo = _split3(x)
    return _dg(m01, hi, _NN) + (_dg(m01, mid, _NN) + _dg(m01, lo, _NN))


def _mm_exact_rhs01(x, m01):
    hi, mid, lo = _split3(x)
    return _dg(hi, m01, _NN) + (_dg(mid, m01, _NN) + _dg(lo, m01, _NN))


def _rmsnorm_mod(x, g, scale, shift):
    y = x * lax.rsqrt(jnp.mean(x * x, axis=-1, keepdims=True) + NORM_EPS)
    return (y * g) * (1.0 + scale) + shift


def _mods_kernel(c_ref, w_ref, b_ref, o_ref):
    c = c_ref[...]
    o_ref[0] = _mm(_silu(c), w_ref[0]) + b_ref[0]


def _mods(cmat, w_ada, b_ada):
    nt = 4
    tn = 6 * D // nt
    return pl.pallas_call(
        _mods_kernel,
        grid=(DEPTH, nt),
        in_specs=[pl.BlockSpec((16, D), lambda l, j: (0, 0)),
                  pl.BlockSpec((1, D, tn), lambda l, j: (l, 0, j)),
                  pl.BlockSpec((1, 1, tn), lambda l, j: (l, 0, j))],
        out_specs=pl.BlockSpec((1, 16, tn), lambda l, j: (l, 0, j)),
        out_shape=jax.ShapeDtypeStruct((DEPTH, 16, 6 * D), F32),
        compiler_params=_cp(("arbitrary", "arbitrary")),
    )(cmat, w_ada, b_ada.reshape(DEPTH, 1, 6 * D))


def _pre_kernel(has_res, *refs):
    if has_res:
        (x_ref, moe_ref, pm_ref, m_ref, g_ref, wl_ref, wr_ref, wg_ref,
         xo_ref, lru_ref, rw_ref, gla_ref) = refs
        x = x_ref[0] + pm_ref[0, 5:6, :] * moe_ref[0]
        xo_ref[0] = x
    else:
        x_ref, m_ref, g_ref, wl_ref, wr_ref, wg_ref, lru_ref, rw_ref, gla_ref = refs
        x = x_ref[0]
    h = _rmsnorm_mod(x, g_ref[...], m_ref[0, 1:2, :], m_ref[0, 0:1, :]).astype(BF16)
    lru_ref[0] = _dg(h, wl_ref[...], _NN)
    rw_ref[0] = _dg(h, wr_ref[...], _NN)
    gla_ref[0] = _dg(h, wg_ref[...], _NN)


def _pre(x, moe, prev_mods, mods, g, wl, wr, wg, tl=512):
    ns, sl, _ = x.shape
    has_res = moe is not None
    tok = lambda c: pl.BlockSpec((1, tl, c), lambda s, t: (s, t, 0))
    mspec = pl.BlockSpec((1, 6, D), lambda s, t: (s, 0, 0))
    full = lambda a: pl.BlockSpec(a.shape, lambda s, t: (0,) * a.ndim)
    ins = [x] + ([moe, prev_mods] if has_res else []) + [mods, g, wl, wr, wg]
    in_specs = [tok(D)] + ([tok(D), mspec] if has_res else []) + [mspec, full(g), full(wl), full(wr), full(wg)]
    outs = [jax.ShapeDtypeStruct((ns, sl, c), F32) for c in (2 * D_LRU, RW_COLS, GLA_PCOLS)]
    out_specs = [tok(2 * D_LRU), tok(RW_COLS), tok(GLA_PCOLS)]
    if has_res:
        outs = [jax.ShapeDtypeStruct((ns, sl, D), F32)] + outs
        out_specs = [tok(D)] + out_specs
    res = pl.pallas_call(
        functools.partial(_pre_kernel, has_res),
        grid=(ns, sl // tl), in_specs=in_specs, out_specs=out_specs, out_shape=outs,
        compiler_params=_cp(("arbitrary", "arbitrary")),
    )(*ins)
    if has_res:
        return res
    return [x] + list(res)


def _lru_kernel(tl, seg0, segn, in_ref, cw_ref, cb_ref, wb_ref, bias_ref, lam_ref, h0_ref,
                o_ref, hfin_ref, xp_ref, hf_ref):
    s = pl.program_id(0)
    sl = in_ref.shape[1]
    nt = sl // tl
    seg = jnp.where(s == 0, seg0, segn)
    c = D_LRU

    xp_ref[0:8, :] = jnp.zeros((8, c), F32)
    xp_ref[sl + 8:sl + 16, :] = jnp.zeros((8, c), F32)
    xp_ref[8:sl + 8, :] = in_ref[0, :, 0:c]
    hfin_ref[...] = jnp.zeros(hfin_ref.shape, F32)

    cw = cw_ref[...]
    cb = cb_ref[...]
    sp = _softplus(-lam_ref[...])
    rid = lax.broadcasted_iota(jnp.int32, (tl, 1), 0)
    wid = lax.broadcasted_iota(jnp.int32, (tl + 16, 1), 0)

    def gates(i, d):
        t0 = pl.multiple_of(i * tl, tl)
        pos = i % seg
        win = xp_ref[pl.ds(t0, tl + 16), :]
        lo = jnp.where(pos == 0, 8, 0)
        hi = jnp.where(pos == seg - 1, tl + 8, tl + 16)
        win = jnp.where(wid >= lo, jnp.where(wid < hi, win, 0.0), 0.0)
        xc = cb + cw[0:1] * win[6:6 + tl] + cw[1:2] * win[7:7 + tl] + cw[2:3] * win[8:8 + tl] + cw[3:4] * win[9:9 + tl]
        gt = _dg(xc.astype(BF16), wb_ref[:, 2 * c * d:2 * c * (d + 1)], _NN) + bias_ref[:, 2 * c * d:2 * c * (d + 1)]
        r = _sigmoid(gt[:, 0:c])
        ig = _sigmoid(gt[:, c:2 * c])
        log_a = (-LRU_C) * r * sp[d:d + 1]
        a = jnp.exp(log_a)
        u = jnp.sqrt(_one_minus_exp(2.0 * log_a)) * (ig * xc)
        return t0, pos, a, u

    def scan(a, u, rev):
        k = 1
        while k < tl:
            if rev:
                ok = rid < tl - k
                a_s = jnp.where(ok, pltpu.roll(a, tl - k, 0), 1.0)
                u_s = jnp.where(ok, pltpu.roll(u, tl - k, 0), 0.0)
            else:
                ok = rid >= k
                a_s = jnp.where(ok, pltpu.roll(a, k, 0), 1.0)
                u_s = jnp.where(ok, pltpu.roll(u, k, 0), 0.0)
            u = a * u_s + u
            a = a * a_s
            k *= 2
        return a, u

    def fwd(i, carry):
        t0, pos, a, u = gates(i, 0)
        a, u = scan(a, u, False)
        carry = jnp.where(pos == 0, h0_ref[0, 0:1, :], carry)
        h = a * carry + u
        hf_ref[pl.ds(t0, tl), :] = h
        last = h[tl - 1:tl, :]

        @pl.when(pos == seg - 1)
        def _():
            hfin_ref[0, 0, pl.ds(i // seg, 1), :] = last
        return last

    lax.fori_loop(0, nt, fwd, jnp.zeros((1, c), F32))

    def bwd(j, carry):
        i = nt - 1 - j
        t0, pos, a, u = gates(i, 1)
        a, u = scan(a, u, True)
        carry = jnp.where(pos == seg - 1, h0_ref[0, 1:2, :], carry)
        h = a * carry + u
        y = in_ref[0, pl.ds(t0, tl), c:2 * c]
        o_ref[0, pl.ds(t0, tl), :] = (hf_ref[pl.ds(t0, tl), :] + h) * _gelu_tanh(y)
        first = h[0:1, :]

        @pl.when(pos == 0)
        def _():
            hfin_ref[0, 1, pl.ds(i // seg, 1), :] = first
        return first

    lax.fori_loop(0, nt, bwd, jnp.zeros((1, c), F32))


def _lru(lru_in, cw, cb, wblk, bias, lam, h0, p_l, nreq0, tl=256):
    ns, sl, _ = lru_in.shape
    full = lambda a: pl.BlockSpec(a.shape, lambda s: (0,) * a.ndim)
    return pl.pallas_call(
        functools.partial(_lru_kernel, tl, p_l // tl, sl // tl),
        grid=(ns,),
        in_specs=[pl.BlockSpec((1, sl, 2 * D_LRU), lambda s: (s, 0, 0)),
                  full(cw), full(cb), full(wblk), full(bias), full(lam),
                  pl.BlockSpec((1, 2, D_LRU), lambda s: (s, 0, 0))],
        out_specs=[pl.BlockSpec((1, sl, D_LRU), lambda s: (s, 0, 0)),
                   pl.BlockSpec((1, 2, nreq0, D_LRU), lambda s: (s, 0, 0, 0))],
        out_shape=[jax.ShapeDtypeStruct((ns, sl, D_LRU), F32),
                   jax.ShapeDtypeStruct((ns, 2, nreq0, D_LRU), F32)],
        scratch_shapes=[pltpu.VMEM((sl + 16, D_LRU), F32), pltpu.VMEM((sl, D_LRU), F32)],
        compiler_params=_cp(("arbitrary",)),
    )(lru_in, cw, cb, wblk, bias, lam, h0)


RW_P_INV = 3
RW_P_ATT = 1


def _rwkv_kernel(reverse, epilogue, seg0, segn, *refs):
    if epilogue:
        (cols_ref, prev_ref, next_ref, mu_ref, w0_ref, w2_ref, a0_ref, a2_ref, g2_ref, kkw_ref, ka_ref,
         hones_ref, tri_ref, s0_ref, of_ref, a0o_ref, a2o_ref, rk_ref, lnw_ref, lnb_ref,
         o_ref, sfin_ref, s_scr) = refs
    else:
        (cols_ref, prev_ref, next_ref, mu_ref, w0_ref, w2_ref, a0_ref, a2_ref, g2_ref, kkw_ref, ka_ref,
         hones_ref, tri_ref, s0_ref, o_ref, sfin_ref, s_scr) = refs
    s = pl.program_id(0)
    c = pl.program_id(1)
    nch = pl.num_programs(1)
    cc = nch - 1 - c if reverse else c
    seg = jnp.where(s == 0, seg0, segn)
    pos = cc % seg
    first = pos == 0
    last = pos == seg - 1
    start = last if reverse else first

    @pl.when(start)
    def _():
        s_scr[...] = s0_ref[0]

    cols = cols_ref[0]
    prow = jnp.where(first, 0.0, prev_ref[0, 7:8, :])
    nrow = jnp.where(last, 0.0, next_ref[0, 0:1, :])
    rid = lax.broadcasted_iota(jnp.int32, (CH, 1), 0)
    prev = jnp.where(rid == 0, prow, pltpu.roll(cols, 1, 0))
    nxt = jnp.where(rid == CH - 1, nrow, pltpu.roll(cols, CH - 1, 0))
    xs = cols + mu_ref[0:1, :] * (prev - cols) + mu_ref[1:2, :] * (nxt - cols)

    w = D_RW
    r = xs[:, 0:w]
    k = xs[:, w:2 * w]
    v = xs[:, 2 * w:3 * w]
    wd = xs[:, 3 * w:3 * w + 128]
    ad = xs[:, 3 * w + 128:3 * w + 256]
    gd = xs[:, 3 * w + 256:3 * w + 384]

    hones = hones_ref[...]
    wlog = -_softplus(-(w0_ref[...] + _mm(jnp.tanh(wd), w2_ref[...]))) - 0.5
    lw = -jnp.exp(wlog)
    alpha = _sigmoid(a0_ref[...] + _mm(ad, a2_ref[...]))
    kk = k * kkw_ref[...]
    kk = kk * lax.rsqrt(_mm_exact_rhs01(kk * kk, hones) + 1e-12)
    kd = k * (1.0 + (alpha - 1.0) * ka_ref[...])

    bw = _mm_exact_lhs01(tri_ref[...], lw)
    tot = bw[0:1, :] if reverse else bw[CH - 1:CH, :]
    rt = r * jnp.exp(bw)
    at = -kk * jnp.exp(bw - lw)
    en = jnp.exp(-bw)
    kal = kk * alpha
    bt = kal * en
    kt = kd * en
    ec = jnp.exp(tot - bw)
    bh = kal * ec
    kh = kd * ec
    wc = jnp.exp(tot)

    ti = lax.broadcasted_iota(jnp.int32, (CH, CH), 0)
    si = lax.broadcasted_iota(jnp.int32, (CH, CH), 1)
    strict = (si > ti) if reverse else (si < ti)
    incl = (si >= ti) if reverse else (si <= ti)
    eye = jnp.where(si == ti, 1.0, 0.0)

    outs = []
    for h in range(RW_H):
        sl = slice(h * RW_HD, (h + 1) * RW_HD)
        st = s_scr[h]
        at_h, rt_h, bt_h, kt_h, v_h = at[:, sl], rt[:, sl], bt[:, sl], kt[:, sl], v[:, sl]
        n = jnp.where(strict, _mm(at_h, bt_h, _NT, RW_P_INV), 0.0)
        aak = jnp.where(strict, _mm(at_h, kt_h, _NT, RW_P_ATT), 0.0)
        mrb = jnp.where(incl, _mm(rt_h, bt_h, _NT, RW_P_ATT), 0.0)
        mrk = jnp.where(incl, _mm(rt_h, kt_h, _NT, RW_P_ATT), 0.0)
        tinv = eye + n
        npow = n
        p = 2
        while p < CH:
            npow = _mm(npow, npow, _NN, RW_P_INV)
            tinv = tinv + _mm(tinv, npow, _NN, RW_P_INV)
            p *= 2
        y = _mm(at_h, st, _NT, RW_P_INV) + _mm(aak, v_h, _NN, RW_P_ATT)
        u = _mm(tinv, y, _NN, RW_P_INV)
        o_h = _mm(rt_h, st, _NT, RW_P_ATT) + _mm(mrb, u, _NN, RW_P_ATT) + _mm(mrk, v_h, _NN, RW_P_ATT)
        s_new = st * wc[:, sl] + _mm(u, bh[:, sl], _TN, RW_P_INV) + _mm(v_h, kh[:, sl], _TN, RW_P_ATT)
        s_scr[h] = s_new
        sfin_ref[0, h] = s_new
        outs.append(o_h)
    o = jnp.concatenate(outs, axis=1)

    if epilogue:
        o = of_ref[0] + o
        inv = 1.0 / RW_HD
        mean = _mm_exact_rhs01(o, hones) * inv
        dlt = o - mean
        var = _mm_exact_rhs01(dlt * dlt, hones) * inv
        on = dlt * lax.rsqrt(var + RW_GN_EPS) * lnw_ref[...] + lnb_ref[...]
        alpha_o = _sigmoid(a0o_ref[...] + _mm(ad, a2o_ref[...]))
        kd_o = k * (1.0 + (alpha_o - 1.0) * ka_ref[...])
        d_here, d_oth = (1, 0) if reverse else (0, 1)
        rk = rk_ref[...]
        bonus = _mm_exact_rhs01(r * (kd * rk[d_here:d_here + 1] + kd_o * rk[d_oth:d_oth + 1]), hones) * v
        g = _mm(_sigmoid(gd), g2_ref[...])
        o = (on + bonus) * g
    o_ref[0] = o


def _rwkv_dir(reverse, cols, prm, s0, of, p_l, nreq):
    ns, sl, _ = cols.shape
    nch = sl // CH
    d = 1 if reverse else 0
    epilogue = of is not None
    cidx = (lambda c: nch - 1 - c) if reverse else (lambda c: c)
    nb8 = sl // 8
    full = lambda a: pl.BlockSpec(a.shape, lambda s, c: (0,) * a.ndim)
    seg0 = p_l // CH

    def req(s, c):
        return jnp.where(s == 0, cidx(c) // seg0, nreq - ns + s)

    ins = [cols, cols, cols, prm['mu'], prm['w0'][d], prm['w2'][d], prm['a0'][d], prm['a2'][d], prm['g2'],
           prm['kkw'], prm['ka'], prm['hones'], prm['tri'][d], s0]
    in_specs = [pl.BlockSpec((1, CH, RW_COLS), lambda s, c: (s, cidx(c), 0)),
                pl.BlockSpec((1, 8, RW_COLS), lambda s, c: (s, jnp.maximum(cidx(c) * 8 - 1, 0), 0)),
                pl.BlockSpec((1, 8, RW_COLS), lambda s, c: (s, jnp.minimum(cidx(c) * 8 + 8, nb8 - 1), 0))]
    in_specs += [full(a) for a in ins[3:13]]
    in_specs += [pl.BlockSpec((1, RW_H, RW_HD, RW_HD), lambda s, c: (s, 0, 0, 0))]
    if epilogue:
        ins += [of, prm['a0'][1 - d], prm['a2'][1 - d], prm['rk'], prm['lnw'], prm['lnb']]
        in_specs += [pl.BlockSpec((1, CH, D_RW), lambda s, c: (s, cidx(c), 0))]
        in_specs += [full(a) for a in ins[15:]]
    return pl.pallas_call(
        functools.partial(_rwkv_kernel, reverse, epilogue, seg0, nch),
        grid=(ns, nch), in_specs=in_specs,
        out_specs=[pl.BlockSpec((1, CH, D_RW), lambda s, c: (s, cidx(c), 0)),
                   pl.BlockSpec((1, RW_H, RW_HD, RW_HD), lambda s, c: (req(s, c), 0, 0, 0))],
        out_shape=[jax.ShapeDtypeStruct((ns, sl, D_RW), F32),
                   jax.ShapeDtypeStruct((nreq, RW_H, RW_HD, RW_HD), F32)],
        scratch_shapes=[pltpu.VMEM((RW_H, RW_HD, RW_HD), F32)],
        compiler_params=_cp(("arbitrary", "arbitrary")),
    )(*ins)


def _gla_kernel(reverse, epilogue, seg, *refs):
    if epilogue:
        (cols_ref, wg2_ref, bg_ref, tri_ref, s0_ref, of_ref, nw_ref, prev_ref, o_ref, sfin_ref, s_scr) = refs
    else:
        (cols_ref, wg2_ref, bg_ref, tri_ref, s0_ref, prev_ref, o_ref, sfin_ref, s_scr) = refs
    del prev_ref
    c = pl.program_id(1)
    nch = pl.num_programs(1)
    cc = nch - 1 - c if reverse else c
    pos = cc % seg
    start = (pos == seg - 1) if reverse else (pos == 0)

    @pl.when(start)
    def _():
        s_scr[...] = s0_ref[0]

    cols = cols_ref[0]
    q = cols[:, 0:GLA_QW] * (GLA_DK ** -0.5)
    k = cols[:, GLA_QW:2 * GLA_QW]
    v = cols[:, 2 * GLA_QW:2 * GLA_QW + GLA_VW]
    gd = cols[:, 2 * GLA_QW + 2 * GLA_VW:]
    z = _mm(gd, wg2_ref[...]) + bg_ref[...]
    la = -_softplus(-z) * (1.0 / GLA_TAU)
    b = _mm_exact_lhs01(tri_ref[...], la)
    tot = b[0:1, :] if reverse else b[CH - 1:CH, :]
    qt = q * jnp.exp(b)
    kt = k * jnp.exp(-b)
    kh = k * jnp.exp(tot - b)
    wc = jnp.exp(tot)
    ti = lax.broadcasted_iota(jnp.int32, (CH, CH), 0)
    si = lax.broadcasted_iota(jnp.int32, (CH, CH), 1)
    incl = (si >= ti) if reverse else (si <= ti)
    outs = []
    for h in range(GLA_H):
        sk = slice(h * GLA_KP, (h + 1) * GLA_KP)
        sv = slice(h * GLA_VP, (h + 1) * GLA_VP)
        st = s_scr[h]
        att = jnp.where(incl, _mm(qt[:, sk], kt[:, sk], _NT), 0.0)
        o_h = _mm(att, v[:, sv]) + _mm(qt[:, sk], st, _NT)
        s_new = st * wc[:, sk] + _mm(v[:, sv], kh[:, sk], _TN)
        s_scr[h] = s_new
        sfin_ref[0, h] = s_new
        outs.append(o_h)
    o = jnp.concatenate(outs, axis=1)
    if epilogue:
        o = of_ref[0] + o
        g = cols[:, 2 * GLA_QW + GLA_VW:2 * GLA_QW + 2 * GLA_VW]
        parts = []
        for h in range(GLA_H):
            oh = o[:, h * GLA_VP:(h + 1) * GLA_VP]
            ms = jnp.sum(oh * oh, axis=-1, keepdims=True) * (1.0 / GLA_DV)
            parts.append(oh * lax.rsqrt(ms + NORM_EPS))
        o = jnp.concatenate(parts, axis=1) * nw_ref[...] * _silu(g)
    o_ref[0] = o


def _gla_dir(reverse, colmajor, cols, prm, s0, of, prev_out, seg, slab0, nb):
    ns, sl, _ = cols.shape
    nch = sl // CH
    d = 1 if reverse else 0
    epilogue = of is not None
    cidx = (lambda c: nch - 1 - c) if reverse else (lambda c: c)
    nreq = nb * (nch // seg)
    if colmajor:
        assert sl == GRID_W * CH
        view = lambda a: a.reshape(ns, CH, GRID_W * a.shape[-1])
        tok = lambda w: pl.BlockSpec((1, CH, w), lambda s, c: (s + slab0, 0, cidx(c)))
    else:
        view = lambda a: a
        tok = lambda w: pl.BlockSpec((1, CH, w), lambda s, c: (s + slab0, cidx(c), 0))
    full = lambda a: pl.BlockSpec(a.shape, lambda s, c: (0,) * a.ndim)
    ins = [view(cols), prm['wg2'][d], prm['bg'][d], prm['tri'][d], s0]
    in_specs = [tok(GLA_PCOLS), full(ins[1]), full(ins[2]), full(ins[3]),
                pl.BlockSpec((1, GLA_H, GLA_VP, GLA_KP), lambda s, c: (s * (nch // seg) + cidx(c) // seg, 0, 0, 0))]
    if epilogue:
        ins += [view(of), prm['nw']]
        in_specs += [tok(GLA_VW), full(prm['nw'])]
    out_shape = [jax.ShapeDtypeStruct((ns, CH, GRID_W * GLA_VW) if colmajor else (ns, sl, GLA_VW), F32),
                 jax.ShapeDtypeStruct((nreq, GLA_H, GLA_VP, GLA_KP), F32)]
    aliases = {}
    if prev_out is None:
        ins.append(jnp.zeros((8, 128), F32))
        in_specs.append(pl.BlockSpec((8, 128), lambda s, c: (0, 0)))
    else:
        ins.append(view(prev_out))
        in_specs.append(pl.BlockSpec(memory_space=pl.ANY))
        aliases = {len(ins) - 1: 0}
    o, sfin = pl.pallas_call(
        functools.partial(_gla_kernel, reverse, epilogue, seg),
        grid=(nb, nch), in_specs=in_specs,
        out_specs=[tok(GLA_VW),
                   pl.BlockSpec((1, GLA_H, GLA_VP, GLA_KP), lambda s, c: (s * (nch // seg) + cidx(c) // seg, 0, 0, 0))],
        out_shape=out_shape,
        scratch_shapes=[pltpu.VMEM((GLA_H, GLA_VP, GLA_KP), F32)],
        input_output_aliases=aliases,
        compiler_params=_cp(("arbitrary", "arbitrary")),
    )(*ins)
    return o.reshape(ns, sl, GLA_VW), sfin


def _post_kernel(x_ref, ol_ref, or_ref, og_ref, m_ref, g_ref, w1_ref, w2_ref, w3_ref, wrt_ref,
                 x1_ref, h2_ref, pt_ref):
    att = (_dg(ol_ref[0].astype(BF16), w1_ref[...], _NN) + _dg(or_ref[0].astype(BF16), w2_ref[...], _NN)
           + _dg(og_ref[0].astype(BF16), w3_ref[...], _NN))
    x1 = x_ref[0] + m_ref[0, 2:3, :] * att
    x1_ref[0] = x1
    h2 = _rmsnorm_mod(x1, g_ref[...], m_ref[0, 4:5, :], m_ref[0, 3:4, :])
    h2_ref[0] = h2
    logits = lax.dot_general(wrt_ref[...], h2, _NT, precision=lax.Precision.HIGHEST,
                             preferred_element_type=F32)
    mx = jnp.max(logits, axis=0, keepdims=True)
    e = jnp.exp(logits - mx)
    pt_ref[0] = e / jnp.sum(e, axis=0, keepdims=True)


def _post(x, o_lru, o_rw, o_gla, mods, g, w1, w2, w3, wrt, tl=512):
    ns, sl, _ = x.shape
    tok = lambda c: pl.BlockSpec((1, tl, c), lambda s, t: (s, t, 0))
    full = lambda a: pl.BlockSpec(a.shape, lambda s, t: (0,) * a.ndim)
    return pl.pallas_call(
        _post_kernel,
        grid=(ns, sl // tl),
        in_specs=[tok(D), tok(D_LRU), tok(D_RW), tok(GLA_VW), pl.BlockSpec((1, 6, D), lambda s, t: (s, 0, 0)),
                  full(g), full(w1), full(w2), full(w3), full(wrt)],
        out_specs=[tok(D), tok(D), pl.BlockSpec((1, N_EXPERTS, tl), lambda s, t: (s, 0, t))],
        out_shape=[jax.ShapeDtypeStruct((ns, sl, D), F32), jax.ShapeDtypeStruct((ns, sl, D), F32),
                   jax.ShapeDtypeStruct((ns, N_EXPERTS, sl), F32)],
        compiler_params=_cp(("arbitrary", "arbitrary")),
    )(x, o_lru, o_rw, o_gla, mods, g, w1, w2, w3, wrt)


def _gather_kernel(cap, idx_ref, h_ref, o_ref, buf):
    s = pl.program_id(0)
    e = pl.program_id(1)
    base = (s * N_EXPERTS + e) * cap

    def body(j, _):
        row = idx_ref[base + j]
        buf[pl.ds(j, 1), :] = h_ref[0, pl.ds(row, 1), :]
        return 0

    lax.fori_loop(0, cap, body, 0)
    o_ref[0, 0] = buf[...].astype(BF16)


def _moe_gather(h2, idx_flat, cap):
    ns, sl, _ = h2.shape
    return pl.pallas_call(
        functools.partial(_gather_kernel, cap),
        grid_spec=pltpu.PrefetchScalarGridSpec(
            num_scalar_prefetch=1, grid=(ns, N_EXPERTS),
            in_specs=[pl.BlockSpec((1, sl, D), lambda s, e, idx: (s, 0, 0))],
            out_specs=pl.BlockSpec((1, 1, cap, D), lambda s, e, idx: (e, s, 0, 0)),
            scratch_shapes=[pltpu.VMEM((cap, D), F32)]),
        out_shape=jax.ShapeDtypeStruct((N_EXPERTS, ns, cap, D), BF16),
        compiler_params=_cp(("arbitrary", "arbitrary")),
    )(idx_flat, h2)


def _expert_kernel(x_ref, wg_ref, wu_ref, wd_ref, o_ref, wgb, wub, wdb):
    @pl.when(pl.program_id(1) == 0)
    def _():
        wgb[...] = wg_ref[0].astype(BF16)
        wub[...] = wu_ref[0].astype(BF16)
        wdb[...] = wd_ref[0].astype(BF16)

    x = x_ref[0, 0]
    hid = _silu(_dg(x, wgb[...], _NN)) * _dg(x, wub[...], _NN)
    o_ref[0, 0] = _dg(hid.astype(BF16), wdb[...], _NN)


def _moe_experts(xg, wg, wu, wd, l):
    ne, ns, cap, _ = xg.shape
    de = wg.shape[-1]
    wspec = lambda a: pl.BlockSpec((None, 1) + a.shape[2:], lambda e, t: (l, e, 0, 0))
    return pl.pallas_call(
        _expert_kernel,
        grid=(ne, ns),
        in_specs=[pl.BlockSpec((1, 1, cap, D), lambda e, t: (e, t, 0, 0)), wspec(wg), wspec(wu), wspec(wd)],
        out_specs=pl.BlockSpec((1, 1, cap, D), lambda e, t: (e, t, 0, 0)),
        out_shape=jax.ShapeDtypeStruct((ne, ns, cap, D), F32),
        scratch_shapes=[pltpu.VMEM((D, de), BF16), pltpu.VMEM((D, de), BF16), pltpu.VMEM((de, D), BF16)],
        compiler_params=_cp(("arbitrary", "arbitrary")),
    )(xg, wg, wu, wd)


def _scatter_kernel(cap, idx_ref, gate_ref, y_ref, o_ref):
    s = pl.program_id(0)
    e = pl.program_id(1)
    base = (s * N_EXPERTS + e) * cap

    @pl.when(e == 0)
    def _():
        o_ref[...] = jnp.zeros(o_ref.shape, F32)

    def body(j, _):
        row = idx_ref[base + j]
        gt = gate_ref[base + j]
        o_ref[0, pl.ds(row, 1), :] = o_ref[0, pl.ds(row, 1), :] + gt * y_ref[0, 0, pl.ds(j, 1), :]
        return 0

    lax.fori_loop(0, cap, body, 0)


def _moe_scatter(yo, idx_flat, gate_flat, sl):
    ne, ns, cap, _ = yo.shape
    return pl.pallas_call(
        functools.partial(_scatter_kernel, cap),
        grid_spec=pltpu.PrefetchScalarGridSpec(
            num_scalar_prefetch=2, grid=(ns, ne),
            in_specs=[pl.BlockSpec((1, 1, cap, D), lambda s, e, i, g: (e, s, 0, 0))],
            out_specs=pl.BlockSpec((1, sl, D), lambda s, e, i, g: (s, 0, 0))),
        out_shape=jax.ShapeDtypeStruct((ns, sl, D), F32),
        compiler_params=_cp(("arbitrary", "arbitrary")),
    )(idx_flat, gate_flat, yo)


def _route(probs_t, p_b, p_l):
    ns, ne, sl = probs_t.shape
    cap_p = 2 * p_l // ne
    pp = probs_t[0].reshape(ne, p_b, p_l)
    gp, ip = lax.top_k(pp, cap_p)
    ip = ip + (jnp.arange(p_b, dtype=jnp.int32) * p_l)[None, :, None]
    cap = 2 * sl // ne
    gs, is_ = lax.top_k(probs_t[1:], cap)
    idx = jnp.concatenate([ip.reshape(1, ne, p_b * cap_p), is_], axis=0).astype(jnp.int32)
    gates = jnp.concatenate([gp.reshape(1, ne, p_b * cap_p), gs], axis=0)
    return idx.reshape(-1), gates.reshape(-1), cap


def _final_kernel(x_ref, moe_ref, m_ref, g_ref, o_ref):
    x = x_ref[0] + m_ref[0, 5:6, :] * moe_ref[0]
    o_ref[0] = (x * lax.rsqrt(jnp.mean(x * x, axis=-1, keepdims=True) + NORM_EPS)) * g_ref[...]


def _final(x, moe, mods, g, slab0, nb, tl=512):
    ns, sl, _ = x.shape
    tok = pl.BlockSpec((1, tl, D), lambda s, t: (s + slab0, t, 0))
    return pl.pallas_call(
        _final_kernel,
        grid=(nb, sl // tl),
        in_specs=[tok, tok, pl.BlockSpec((1, 6, D), lambda s, t: (s + slab0, 0, 0)),
                  pl.BlockSpec(g.shape, lambda s, t: (0, 0))],
        out_specs=pl.BlockSpec((1, tl, D), lambda s, t: (s, t, 0)),
        out_shape=jax.ShapeDtypeStruct((nb, sl, D), F32),
        compiler_params=_cp(("arbitrary", "arbitrary")),
    )(x, moe, mods, g)


def _gla_col_src():
    src = np.full((GLA_PCOLS,), -1, np.int64)
    for h in range(GLA_H):
        for j in range(GLA_DK):
            src[h * GLA_KP + j] = h * GLA_DK + j
            src[GLA_QW + h * GLA_KP + j] = GLA_H * GLA_DK + h * GLA_DK + j
        for j in range(GLA_DV):
            src[2 * GLA_QW + h * GLA_VP + j] = 2 * GLA_H * GLA_DK + h * GLA_DV + j
            src[2 * GLA_QW + GLA_VW + h * GLA_VP + j] = 2 * GLA_H * GLA_DK + D_GLA + h * GLA_DV + j
    for j in range(2 * GLA_R):
        src[2 * GLA_QW + 2 * GLA_VW + j] = 2 * GLA_H * GLA_DK + 2 * D_GLA + j
    return src


_GLA_SRC = _gla_col_src()


def _pad_cols(a, src):
    valid = jnp.asarray(src >= 0)
    return jnp.where(valid, jnp.take(a, jnp.asarray(np.maximum(src, 0)), axis=-1), 0.0)


def _key_src():
    src = np.full((GLA_QW,), -1, np.int64)
    for h in range(GLA_H):
        src[h * GLA_KP:h * GLA_KP + GLA_DK] = np.arange(h * GLA_DK, (h + 1) * GLA_DK)
    return src


def _val_src():
    src = np.full((GLA_VW,), -1, np.int64)
    for h in range(GLA_H):
        src[h * GLA_VP:h * GLA_VP + GLA_DV] = np.arange(h * GLA_DV, (h + 1) * GLA_DV)
    return src


_KEY_SRC = _key_src()
_VAL_SRC = _val_src()


def _tri_consts():
    t = np.arange(CH)
    lower = (t[None, :] <= t[:, None]).astype(np.float32)
    return jnp.asarray(lower, BF16), jnp.asarray(lower.T, BF16)


def _layer_params(l, a):
    p = {}
    w_in = a['w_in'][l]
    p['w_lru'] = w_in[:, :2 * D_LRU].astype(BF16)
    p['w_rw'] = w_in[:, 2 * D_LRU:2 * D_LRU + RW_COLS].astype(BF16)
    p['w_gla'] = _pad_cols(w_in[:, 2 * D_LRU + RW_COLS:], _GLA_SRC).astype(BF16)
    blk = jnp.asarray(np.kron(np.eye(LRU_BLOCKS, dtype=np.float32), np.ones((LRU_BW, LRU_BW), np.float32)))

    def dense(wb):
        rows = jnp.concatenate([jnp.tile(wb[n], (1, LRU_BLOCKS)) for n in range(LRU_BLOCKS)], axis=0)
        return rows * blk

    p['lru_wblk'] = jnp.concatenate([dense(a['lru_wa'][l, 0]), dense(a['lru_wx'][l, 0]),
                                     dense(a['lru_wa'][l, 1]), dense(a['lru_wx'][l, 1])], axis=1).astype(BF16)
    p['lru_bias'] = jnp.concatenate([a['lru_ba'][l, 0], a['lru_bx'][l, 0], a['lru_ba'][l, 1], a['lru_bx'][l, 1]])[None, :]
    p['lru_cw'] = a['lru_conv_w'][l]
    p['lru_cb'] = a['lru_conv_b'][l][None, :]
    p['lru_lam'] = a['lru_lambda'][l]
    lower, upper = _tri_consts()
    hones = jnp.asarray(np.kron(np.eye(RW_H, dtype=np.float32), np.ones((RW_HD, RW_HD), np.float32)), BF16)

    def dirpad(w, d):
        z = jnp.zeros_like(w)
        return jnp.concatenate([w, z] if d == 0 else [z, w], axis=0)

    p['rw'] = {
        'mu': a['rw_mu'][l],
        'w0': [a['rw_w0'][l, d][None, :] for d in range(2)],
        'w2': [dirpad(a['rw_w2'][l, d], d) for d in range(2)],
        'a0': [a['rw_a0'][l, d][None, :] for d in range(2)],
        'a2': [dirpad(a['rw_a2'][l, d], d) for d in range(2)],
        'g2': a['rw_g2'][l],
        'kkw': a['rw_kk'][l][None, :],
        'ka': a['rw_ka'][l][None, :],
        'rk': a['rw_rk'][l].reshape(2, D_RW),
        'lnw': a['rw_ln_w'][l][None, :],
        'lnb': a['rw_ln_b'][l][None, :],
        'hones': hones,
        'tri': [lower, upper],
    }

    def gpad(w, d):
        wp = _pad_cols(w, _KEY_SRC)
        out = jnp.zeros((128, GLA_QW), F32)
        return lax.dynamic_update_slice(out, wp, (d * GLA_R, 0))

    p['gla'] = {
        'wg2': [gpad(a['gla_wg2'][l, d], d) for d in range(2)],
        'bg': [_pad_cols(a['gla_bg'][l, d][None, :], _KEY_SRC) for d in range(2)],
        'nw': _pad_cols(a['gla_norm_w'][l].reshape(1, D_GLA), _VAL_SRC),
        'tri': [lower, upper],
    }
    w_out = a['w_out'][l]
    p['wo1'] = w_out[:D_LRU].astype(BF16)
    p['wo2'] = w_out[D_LRU:D_LRU + D_RW].astype(BF16)
    p['wo3'] = _pad_cols(w_out[D_LRU + D_RW:].T, _VAL_SRC).T.astype(BF16)
    p['wrt'] = a['w_router'][l].T
    p['norm1'] = a['norm1_g'][l][None, :]
    p['norm2'] = a['norm2_g'][l][None, :]
    return p


def _gla_state_in(sg):
    st = jnp.swapaxes(sg, -1, -2)
    return jnp.pad(st, ((0, 0), (0, 0), (0, GLA_VP - GLA_DV), (0, GLA_KP - GLA_DK)))


def _gla_state_out(st):
    return jnp.swapaxes(st[:, :, :GLA_DV, :GLA_DK], -1, -2)


def _forward(a):
    x_prompt, x_sample = a['x_prompt'], a['x_sample']
    p_b, p_l, _ = x_prompt.shape
    s_b, sl, _ = x_sample.shape
    assert p_b * p_l == sl
    ns = s_b + 1
    nreq = p_b + s_b

    cmat = jnp.zeros((16, D), F32).at[:s_b].set(a['c']).at[s_b].set(a['c_ctx'])
    mods_all = _mods(cmat, a['w_ada'], a['b_ada'])
    order = np.array([s_b] + list(range(s_b)))
    mods = mods_all[:, order].reshape(DEPTH, ns, 6, D)

    x = jnp.concatenate([x_prompt.reshape(1, sl, D), x_sample], axis=0)
    moe = None
    st_lru, st_rw, st_gla = [], [], []
    for l in range(DEPTH):
        p = _layer_params(l, a)
        x, lru_in, rw_cols, gla_cols = _pre(x, moe, mods[l - 1] if l else None, mods[l], p['norm1'],
                                            p['w_lru'], p['w_rw'], p['w_gla'])
        h0 = jnp.concatenate([jnp.zeros((1, 2, D_LRU), F32), a['state_lru'][:, l]], axis=0)
        o_lru, hfin = _lru(lru_in, p['lru_cw'], p['lru_cb'], p['lru_wblk'], p['lru_bias'], p['lru_lam'], h0, p_l, p_b)
        st_lru.append(jnp.swapaxes(hfin[0], 0, 1))
        zs = jnp.zeros((1, RW_H, RW_HD, RW_HD), F32)
        s0f = jnp.concatenate([zs, a['state_rwkv'][:, l, 0]], axis=0)
        s0b = jnp.concatenate([zs, a['state_rwkv'][:, l, 1]], axis=0)
        of, sf = _rwkv_dir(False, rw_cols, p['rw'], s0f, None, p_l, nreq)
        o_rw, sb = _rwkv_dir(True, rw_cols, p['rw'], s0b, of, p_l, nreq)
        st_rw.append(jnp.stack([sf[:p_b], sb[:p_b]], axis=1))
        zg = jnp.zeros((p_b, GLA_H, GLA_VP, GLA_KP), F32)
        gp_f, gsf = _gla_dir(False, False, gla_cols, p['gla'], zg, None, None, p_l // CH, 0, 1)
        gs_f, _ = _gla_dir(False, True, gla_cols, p['gla'], _gla_state_in(a['state_gla'][:, l, 0]), None, gp_f,
                           sl // CH, 1, s_b)
        gp, gsb = _gla_dir(True, False, gla_cols, p['gla'], zg, gs_f, None, p_l // CH, 0, 1)
        o_gla, _ = _gla_dir(True, True, gla_cols, p['gla'], _gla_state_in(a['state_gla'][:, l, 1]), gs_f, gp,
                            sl // CH, 1, s_b)
        st_gla.append(jnp.stack([_gla_state_out(gsf), _gla_state_out(gsb)], axis=1))
        x, h2, probs_t = _post(x, o_lru, o_rw, o_gla, mods[l], p['norm2'], p['wo1'], p['wo2'], p['wo3'], p['wrt'])
        idx, gates, cap = _route(probs_t, p_b, p_l)
        xg = _moe_gather(h2, idx, cap)
        yo = _moe_experts(xg, a['w_gate'], a['w_up'], a['w_down'], l)
        moe = _moe_scatter(yo, idx, gates, sl)
    fg = a['final_norm_g'][None, :]
    y_prompt = _final(x, moe, mods[DEPTH - 1], fg, 0, 1).reshape(p_b, p_l, D)
    y_sample = _final(x, moe, mods[DEPTH - 1], fg, 1, s_b)
    return (y_prompt, y_sample, jnp.stack(st_lru, axis=1), jnp.stack(st_rw, axis=1), jnp.stack(st_gla, axis=1))


def kernel(x_prompt, x_sample, c, state_lru, state_rwkv, state_gla, c_ctx, norm1_g, norm2_g, w_ada, b_ada, w_in, lru_conv_w, lru_conv_b, lru_wa, lru_ba, lru_wx, lru_bx, lru_lambda, rw_mu, rw_w0, rw_w2, rw_a0, rw_a2, rw_g2, rw_kk, rw_ka, rw_rk, rw_ln_w, rw_ln_b, gla_wg2, gla_bg, gla_norm_w, w_out, w_router, w_gate, w_up, w_down, final_norm_g):
    return _forward(dict(locals()))
```

```python
import functools

import numpy as np
import jax
import jax.numpy as jnp
from jax import lax
from jax.experimental import pallas as pl
from jax.experimental.pallas import tpu as pltpu

F32 = jnp.float32
BF16 = jnp.bfloat16

D = 1024
DEPTH = 2
NORM_EPS = 1e-6
D_LRU = 256
LRU_BLOCKS = 4
LRU_BW = 64
LRU_C = 8.0
D_RW = 384
RW_H = 6
RW_HD = 64
RW_COLS = 1536
RW_GN_EPS = 64e-5
D_GLA = 384
GLA_H = 4
GLA_DV = 96
GLA_DK = 48
GLA_R = 16
GLA_TAU = 16.0
N_EXPERTS = 16
GRID_W = 64

GLA_KP = 64
GLA_VP = 128
GLA_QW = GLA_H * GLA_KP
GLA_VW = GLA_H * GLA_VP
GLA_PCOLS = 2 * GLA_QW + 2 * GLA_VW + 128

CH = 64
VMEM_LIMIT = 56 * 1024 * 1024


def _cp(sem, vmem=VMEM_LIMIT):
    return pltpu.CompilerParams(dimension_semantics=sem, vmem_limit_bytes=vmem)


def _sigmoid(x):
    return 1.0 / (1.0 + jnp.exp(-x))


def _softplus(x):
    return jnp.maximum(x, 0.0) + jnp.log1p(jnp.exp(-jnp.abs(x)))


def _one_minus_exp(y):
    e = jnp.exp(y)
    em1 = e - 1.0
    small = jnp.where(em1 == 0.0, y, em1 * y / jnp.log(e))
    return -jnp.where(y < -0.5, em1, small)


def _silu(x):
    return x * _sigmoid(x)


def _gelu_tanh(x):
    return 0.5 * x * (1.0 + jnp.tanh(0.7978845608028654 * (x + 0.044715 * (x * x * x))))


_NN = (((1,), (0,)), ((), ()))
_NT = (((1,), (1,)), ((), ()))
_TN = (((0,), (0,)), ((), ()))


def _dg(a, b, dims):
    return lax.dot_general(a, b, dims, preferred_element_type=F32)


def _split2(x):
    hi = x.astype(BF16)
    lo = (x - hi.astype(F32)).astype(BF16)
    return hi, lo


def _split3(x):
    hi = x.astype(BF16)
    r = x - hi.astype(F32)
    mid = r.astype(BF16)
    lo = (r - mid.astype(F32)).astype(BF16)
    return hi, mid, lo


def _mm(a, b, dims=_NN, passes=1):
    if passes == 1:
        return _dg(a.astype(BF16), b.astype(BF16), dims)
    ah, al = _split2(a)
    bh, bl = _split2(b)
    return _dg(ah, bh, dims) + (_dg(ah, bl, dims) + _dg(al, bh, dims))


def _mm_exact_lhs01(m01, x):
    hi, mid, lo = _split3(x)
    return _dg(m01, hi, _NN) + (_dg(m01, mid, _NN) + _dg(m01, lo, _NN))


def _mm_exact_rhs01(x, m01):
    hi, mid, lo = _split3(x)
    return _dg(hi, m01, _NN) + (_dg(mid, m01, _NN) + _dg(lo, m01, _NN))


def _rmsnorm_mod(x, g, scale, shift):
    y = x * lax.rsqrt(jnp.mean(x * x, axis=-1, keepdims=True) + NORM_EPS)
    return (y * g) * (1.0 + scale) + shift


def _mods_kernel(c_ref, w_ref, b_ref, o_ref):
    c = c_ref[...]
    o_ref[0] = _mm(_silu(c), w_ref[0]) + b_ref[0]


def _mods(cmat, w_ada, b_ada):
    nt = 4
    tn = 6 * D // nt
    return pl.pallas_call(
        _mods_kernel,
        grid=(DEPTH, nt),
        in_specs=[pl.BlockSpec((16, D), lambda l, j: (0, 0)),
                  pl.BlockSpec((1, D, tn), lambda l, j: (l, 0, j)),
                  pl.BlockSpec((1, 1, tn), lambda l, j: (l, 0, j))],
        out_specs=pl.BlockSpec((1, 16, tn), lambda l, j: (l, 0, j)),
        out_shape=jax.ShapeDtypeStruct((DEPTH, 16, 6 * D), F32),
        compiler_params=_cp(("arbitrary", "arbitrary")),
    )(cmat, w_ada, b_ada.reshape(DEPTH, 1, 6 * D))


def _pre_kernel(has_res, *refs):
    if has_res:
        (x_ref, moe_ref, pm_ref, m_ref, g_ref, wl_ref, wr_ref, wg_ref,
         xo_ref, lru_ref, rw_ref, gla_ref) = refs
        x = x_ref[0] + pm_ref[0, 5:6, :] * moe_ref[0]
        xo_ref[0] = x
    else:
        x_ref, m_ref, g_ref, wl_ref, wr_ref, wg_ref, lru_ref, rw_ref, gla_ref = refs
        x = x_ref[0]
    h = _rmsnorm_mod(x, g_ref[...], m_ref[0, 1:2, :], m_ref[0, 0:1, :]).astype(BF16)
    lru_ref[0] = _dg(h, wl_ref[...], _NN)
    rw_ref[0] = _dg(h, wr_ref[...], _NN)
    gla_ref[0] = _dg(h, wg_ref[...], _NN)


def _pre(x, moe, prev_mods, mods, g, wl, wr, wg, tl=512):
    ns, sl, _ = x.shape
    has_res = moe is not None
    tok = lambda c: pl.BlockSpec((1, tl, c), lambda s, t: (s, t, 0))
    mspec = pl.BlockSpec((1, 6, D), lambda s, t: (s, 0, 0))
    full = lambda a: pl.BlockSpec(a.shape, lambda s, t: (0,) * a.ndim)
    ins = [x] + ([moe, prev_mods] if has_res else []) + [mods, g, wl, wr, wg]
    in_specs = [tok(D)] + ([tok(D), mspec] if has_res else []) + [mspec, full(g), full(wl), full(wr), full(wg)]
    outs = [jax.ShapeDtypeStruct((ns, sl, c), F32) for c in (2 * D_LRU, RW_COLS, GLA_PCOLS)]
    out_specs = [tok(2 * D_LRU), tok(RW_COLS), tok(GLA_PCOLS)]
    if has_res:
        outs = [jax.ShapeDtypeStruct((ns, sl, D), F32)] + outs
        out_specs = [tok(D)] + out_specs
    res = pl.pallas_call(
        functools.partial(_pre_kernel, has_res),
        grid=(ns, sl // tl), in_specs=in_specs, out_specs=out_specs, out_shape=outs,
        compiler_params=_cp(("arbitrary", "arbitrary")),
    )(*ins)
    if has_res:
        return res
    return [x] + list(res)


def _lru_kernel(tl, seg0, segn, in_ref, cw_ref, cb_ref, wb_ref, bias_ref, lam_ref, h0_ref,
                o_ref, hfin_ref, xp_ref, hf_ref):
    s = pl.program_id(0)
    sl = in_ref.shape[1]
    nt = sl // tl
    seg = jnp.where(s == 0, seg0, segn)
    c = D_LRU

    xp_ref[0:8, :] = jnp.zeros((8, c), F32)
    xp_ref[sl + 8:sl + 16, :] = jnp.zeros((8, c), F32)
    xp_ref[8:sl + 8, :] = in_ref[0, :, 0:c]
    hfin_ref[...] = jnp.zeros(hfin_ref.shape, F32)

    cw = cw_ref[...]
    cb = cb_ref[...]
    sp = _softplus(-lam_ref[...])
    rid = lax.broadcasted_iota(jnp.int32, (tl, 1), 0)
    wid = lax.broadcasted_iota(jnp.int32, (tl + 16, 1), 0)

    def gates(i, d):
        t0 = pl.multiple_of(i * tl, tl)
        pos = i % seg
        win = xp_ref[pl.ds(t0, tl + 16), :]
        lo = jnp.where(pos == 0, 8, 0)
        hi = jnp.where(pos == seg - 1, tl + 8, tl + 16)
        win = jnp.where(wid >= lo, jnp.where(wid < hi, win, 0.0), 0.0)
        xc = cb + cw[0:1] * win[6:6 + tl] + cw[1:2] * win[7:7 + tl] + cw[2:3] * win[8:8 + tl] + cw[3:4] * win[9:9 + tl]
        gt = _dg(xc.astype(BF16), wb_ref[:, 2 * c * d:2 * c * (d + 1)], _NN) + bias_ref[:, 2 * c * d:2 * c * (d + 1)]
        r = _sigmoid(gt[:, 0:c])
        ig = _sigmoid(gt[:, c:2 * c])
        log_a = (-LRU_C) * r * sp[d:d + 1]
        a = jnp.exp(log_a)
        u = jnp.sqrt(_one_minus_exp(2.0 * log_a)) * (ig * xc)
        return t0, pos, a, u

    def scan(a, u, rev):
        k = 1
        while k < tl:
            if rev:
                ok = rid < tl - k
                a_s = jnp.where(ok, pltpu.roll(a, tl - k, 0), 1.0)
                u_s = jnp.where(ok, pltpu.roll(u, tl - k, 0), 0.0)
            else:
                ok = rid >= k
                a_s = jnp.where(ok, pltpu.roll(a, k, 0), 1.0)
                u_s = jnp.where(ok, pltpu.roll(u, k, 0), 0.0)
            u = a * u_s + u
            a = a * a_s
            k *= 2
        return a, u

    def fwd(i, carry):
        t0, pos, a, u = gates(i, 0)
        a, u = scan(a, u, False)
        carry = jnp.where(pos == 0, h0_ref[0, 0:1, :], carry)
        h = a * carry + u
        hf_ref[pl.ds(t0, tl), :] = h
        last = h[tl - 1:tl, :]

        @pl.when(pos == seg - 1)
        def _():
            hfin_ref[0, 0, pl.ds(i // seg, 1), :] = last
        return last

    lax.fori_loop(0, nt, fwd, jnp.zeros((1, c), F32))

    def bwd(j, carry):
        i = nt - 1 - j
        t0, pos, a, u = gates(i, 1)
        a, u = scan(a, u, True)
        carry = jnp.where(pos == seg - 1, h0_ref[0, 1:2, :], carry)
        h = a * carry + u
        y = in_ref[0, pl.ds(t0, tl), c:2 * c]
        o_ref[0, pl.ds(t0, tl), :] = (hf_ref[pl.ds(t0, tl), :] + h) * _gelu_tanh(y)
        first = h[0:1, :]

        @pl.when(pos == 0)
        def _():
            hfin_ref[0, 1, pl.ds(i // seg, 1), :] = first
        return first

    lax.fori_loop(0, nt, bwd, jnp.zeros((1, c), F32))


def _lru(lru_in, cw, cb, wblk, bias, lam, h0, p_l, nreq0, tl=256):
    ns, sl, _ = lru_in.shape
    full = lambda a: pl.BlockSpec(a.shape, lambda s: (0,) * a.ndim)
    return pl.pallas_call(
        functools.partial(_lru_kernel, tl, p_l // tl, sl // tl),
        grid=(ns,),
        in_specs=[pl.BlockSpec((1, sl, 2 * D_LRU), lambda s: (s, 0, 0)),
                  full(cw), full(cb), full(wblk), full(bias), full(lam),
                  pl.BlockSpec((1, 2, D_LRU), lambda s: (s, 0, 0))],
        out_specs=[pl.BlockSpec((1, sl, D_LRU), lambda s: (s, 0, 0)),
                   pl.BlockSpec((1, 2, nreq0, D_LRU), lambda s: (s, 0, 0, 0))],
        out_shape=[jax.ShapeDtypeStruct((ns, sl, D_LRU), F32),
                   jax.ShapeDtypeStruct((ns, 2, nreq0, D_LRU), F32)],
        scratch_shapes=[pltpu.VMEM((sl + 16, D_LRU), F32), pltpu.VMEM((sl, D_LRU), F32)],
        compiler_params=_cp(("arbitrary",)),
    )(lru_in, cw, cb, wblk, bias, lam, h0)


RW_NC = 4
RW_P_INV = 3
RW_P_STATE = 3
RW_P_ATT = 1
RW_NP = RW_H // 2


def _bd(x):
    lane = lax.broadcasted_iota(jnp.int32, x.shape, 1)
    return jnp.concatenate([jnp.where(lane < 64, x, 0.0), jnp.where(lane >= 64, x, 0.0)], axis=0)


def _rwkv_kernel(reverse, epilogue, nc, seg0, segn, *refs):
    if epilogue:
        (cols_ref, prev_ref, next_ref, mu_ref, w0_ref, w2_ref, a0_ref, a2_ref, g2_ref, kkw_ref, ka_ref,
         hones_ref, tri_ref, s0_ref, of_ref, a0o_ref, a2o_ref, rk_ref, lnw_ref, lnb_ref,
         o_ref, sfin_ref, s_scr) = refs
    else:
        (cols_ref, prev_ref, next_ref, mu_ref, w0_ref, w2_ref, a0_ref, a2_ref, g2_ref, kkw_ref, ka_ref,
         hones_ref, tri_ref, s0_ref, o_ref, sfin_ref, s_scr) = refs
    tl = nc * CH
    s = pl.program_id(0)
    c = pl.program_id(1)
    nt = pl.num_programs(1)
    tt = nt - 1 - c if reverse else c
    seg = jnp.where(s == 0, seg0, segn)
    pos = tt % seg
    first = pos == 0
    last = pos == seg - 1
    start = last if reverse else first

    @pl.when(start)
    def _():
        s_scr[...] = s0_ref[0]

    cols = cols_ref[0]
    prow = jnp.where(first, 0.0, prev_ref[0, 7:8, :])
    nrow = jnp.where(last, 0.0, next_ref[0, 0:1, :])
    rid = lax.broadcasted_iota(jnp.int32, (tl, 1), 0)
    prev = jnp.where(rid == 0, prow, pltpu.roll(cols, 1, 0))
    nxt = jnp.where(rid == tl - 1, nrow, pltpu.roll(cols, tl - 1, 0))
    xs = cols + mu_ref[0:1, :] * (prev - cols) + mu_ref[1:2, :] * (nxt - cols)

    w = D_RW
    r = xs[:, 0:w]
    k = xs[:, w:2 * w]
    v = xs[:, 2 * w:3 * w]
    wd = xs[:, 3 * w:3 * w + 128]
    ad = xs[:, 3 * w + 128:3 * w + 256]
    gd = xs[:, 3 * w + 256:3 * w + 384]

    hones = hones_ref[...]
    wlog = -_softplus(-(w0_ref[...] + _mm(jnp.tanh(wd), w2_ref[...]))) - 0.5
    lw = -jnp.exp(wlog)
    alpha = _sigmoid(a0_ref[...] + _mm(ad, a2_ref[...]))
    kk = k * kkw_ref[...]
    kk = kk * lax.rsqrt(_mm_exact_rhs01(kk * kk, hones) + 1e-12)
    kd = k * (1.0 + (alpha - 1.0) * ka_ref[...])

    bw = _mm_exact_lhs01(tri_ref[...], lw)
    tots = [bw[j * CH:j * CH + 1, :] if reverse else bw[(j + 1) * CH - 1:(j + 1) * CH, :] for j in range(nc)]
    totb = jnp.concatenate([jnp.broadcast_to(t, (CH, w)) for t in tots], axis=0)
    kal = kk * alpha
    rt = r * jnp.exp(bw)
    at = -kk * jnp.exp(bw - lw)
    en = jnp.exp(-bw)
    bt = kal * en
    kt = kd * en
    ec = jnp.exp(totb - bw)
    bh = kal * ec
    kh = kd * ec

    ti = lax.broadcasted_iota(jnp.int32, (CH, 128), 0)
    si = lax.broadcasted_iota(jnp.int32, (CH, 128), 1) & (CH - 1)
    strict = (si > ti) if reverse else (si < ti)
    incl = (si >= ti) if reverse else (si <= ti)
    eye2 = jnp.where(si == ti, 1.0, 0.0)
    lane = lax.broadcasted_iota(jnp.int32, (CH, 128), 1)
    ri = lax.broadcasted_iota(jnp.int32, (128, 128), 0)
    ci = lax.broadcasted_iota(jnp.int32, (128, 128), 1)
    same_head = (ri < 64) == (ci < 64)
    diag = ri == ci

    units = [(j, p) for j in range(nc) for p in range(RW_NP)]

    def sub(x, u):
        j, p = u
        return x[j * CH:(j + 1) * CH, p * 128:(p + 1) * 128]

    n_, rb_, ak_, rk_ = {}, {}, {}, {}
    for u in units:
        lhs = jnp.concatenate([sub(at, u), sub(rt, u)], axis=0)
        gb = _mm(lhs, _bd(sub(bt, u)), _NT, RW_P_INV)
        gk = _mm(lhs, _bd(sub(kt, u)), _NT, RW_P_ATT)
        n_[u] = jnp.where(strict, gb[0:CH], 0.0)
        rb_[u] = jnp.where(incl, gb[CH:2 * CH], 0.0)
        ak_[u] = jnp.where(strict, gk[0:CH], 0.0)
        rk_[u] = jnp.where(incl, gk[CH:2 * CH], 0.0)
    av_ = {u: _mm(ak_[u], _bd(sub(v, u)), _NN, RW_P_ATT) for u in units}
    tp_ = {u: eye2 + n_[u] for u in units}
    np_ = dict(n_)
    p2 = 2
    while p2 < CH:
        np_ = {u: _mm(np_[u], _bd(np_[u]), _NN, RW_P_INV) for u in units}
        tp_ = {u: tp_[u] + _mm(tp_[u], _bd(np_[u]), _NN, RW_P_INV) for u in units}
        p2 *= 2
    pq_ = {u: _mm(tp_[u], jnp.concatenate([_bd(sub(at, u)), _bd(av_[u])], axis=1), _NN, RW_P_INV) for u in units}
    om_, oi_, phi_, psi_ = {}, {}, {}, {}
    for u in units:
        pp, qq = pq_[u][:, 0:128], pq_[u][:, 128:256]
        x2 = _mm(rb_[u], jnp.concatenate([_bd(pp), _bd(qq)], axis=1), _NN, RW_P_ATT)
        om_[u] = sub(rt, u) + x2[:, 0:128]
        oi_[u] = x2[:, 128:256] + _mm(rk_[u], _bd(sub(v, u)), _NN, RW_P_ATT)
        wc = jnp.exp(tots[u[0]][:, u[1] * 128:(u[1] + 1) * 128])
        phi_[u] = jnp.where(same_head, _mm(pp, sub(bh, u), _TN, RW_P_INV), 0.0) + jnp.where(diag, wc, 0.0)
        rr = _mm(qq, sub(bh, u), _TN, RW_P_INV) + _mm(sub(v, u), sub(kh, u), _TN, RW_P_ATT)
        psi_[u] = jnp.where(lane < 64, rr[0:CH], rr[CH:2 * CH])

    st = [s_scr[p] for p in range(RW_NP)]
    rows = [None] * nc
    for j in (range(nc - 1, -1, -1) if reverse else range(nc)):
        o_j = []
        for p in range(RW_NP):
            u = (j, p)
            o_j.append(_mm(om_[u], _bd(st[p]), _NT, RW_P_ATT) + oi_[u])
            st[p] = _mm(st[p], phi_[u], _NN, RW_P_STATE) + psi_[u]
        rows[j] = jnp.concatenate(o_j, axis=1)
    for p in range(RW_NP):
        s_scr[p] = st[p]
        sfin_ref[0, p] = st[p]
    o = jnp.concatenate(rows, axis=0)

    if epilogue:
        o = of_ref[0] + o
        inv = 1.0 / RW_HD
        mean = _mm_exact_rhs01(o, hones) * inv
        dlt = o - mean
        var = _mm_exact_rhs01(dlt * dlt, hones) * inv
        on = dlt * lax.rsqrt(var + RW_GN_EPS) * lnw_ref[...] + lnb_ref[...]
        alpha_o = _sigmoid(a0o_ref[...] + _mm(ad, a2o_ref[...]))
        kd_o = k * (1.0 + (alpha_o - 1.0) * ka_ref[...])
        d_here, d_oth = (1, 0) if reverse else (0, 1)
        rk = rk_ref[...]
        bonus = _mm_exact_rhs01(r * (kd * rk[d_here:d_here + 1] + kd_o * rk[d_oth:d_oth + 1]), hones) * v
        g = _mm(_sigmoid(gd), g2_ref[...])
        o = (on + bonus) * g
    o_ref[0] = o


def _rwkv_dir(reverse, cols, prm, s0, of, p_l, nreq):
    ns, sl, _ = cols.shape
    tl = RW_NC * CH
    nt = sl // tl
    d = 1 if reverse else 0
    epilogue = of is not None
    tidx = (lambda c: nt - 1 - c) if reverse else (lambda c: c)
    nb8 = sl // 8
    r8 = tl // 8
    full = lambda a: pl.BlockSpec(a.shape, lambda s, c: (0,) * a.ndim)
    seg0 = p_l // tl

    def req(s, c):
        return jnp.where(s == 0, tidx(c) // seg0, nreq - ns + s)

    ins = [cols, cols, cols, prm['mu'], prm['w0'][d], prm['w2'][d], prm['a0'][d], prm['a2'][d], prm['g2'],
           prm['kkw'], prm['ka'], prm['hones'], prm['tri'][d], s0]
    in_specs = [pl.BlockSpec((1, tl, RW_COLS), lambda s, c: (s, tidx(c), 0)),
                pl.BlockSpec((1, 8, RW_COLS), lambda s, c: (s, jnp.maximum(tidx(c) * r8 - 1, 0), 0)),
                pl.BlockSpec((1, 8, RW_COLS), lambda s, c: (s, jnp.minimum(tidx(c) * r8 + r8, nb8 - 1), 0))]
    in_specs += [full(a) for a in ins[3:13]]
    in_specs += [pl.BlockSpec((1, RW_NP, RW_HD, 128), lambda s, c: (s, 0, 0, 0))]
    if epilogue:
        ins += [of, prm['a0'][1 - d], prm['a2'][1 - d], prm['rk'], prm['lnw'], prm['lnb']]
        in_specs += [pl.BlockSpec((1, tl, D_RW), lambda s, c: (s, tidx(c), 0))]
        in_specs += [full(a) for a in ins[15:]]
    return pl.pallas_call(
        functools.partial(_rwkv_kernel, reverse, epilogue, RW_NC, seg0, nt),
        grid=(ns, nt), in_specs=in_specs,
        out_specs=[pl.BlockSpec((1, tl, D_RW), lambda s, c: (s, tidx(c), 0)),
                   pl.BlockSpec((1, RW_NP, RW_HD, 128), lambda s, c: (req(s, c), 0, 0, 0))],
        out_shape=[jax.ShapeDtypeStruct((ns, sl, D_RW), F32),
                   jax.ShapeDtypeStruct((nreq, RW_NP, RW_HD, 128), F32)],
        scratch_shapes=[pltpu.VMEM((RW_NP, RW_HD, 128), F32)],
        compiler_params=_cp(("arbitrary", "arbitrary")),
    )(*ins)


def _rw_pack(st):
    n = st.shape[0]
    return st.reshape(n, RW_NP, 2, RW_HD, RW_HD).transpose(0, 1, 3, 2, 4).reshape(n, RW_NP, RW_HD, 2 * RW_HD)


def _rw_unpack(sp):
    n = sp.shape[0]
    return sp.reshape(n, RW_NP, RW_HD, 2, RW_HD).transpose(0, 1, 3, 2, 4).reshape(n, RW_H, RW_HD, RW_HD)


GLA_NC = 4
GLA_NP = GLA_H // 2
GLA_SAFE = 40.0


def _gla_kernel(reverse, epilogue, colmajor, nc, seg, *refs):
    if epilogue:
        (cols_ref, wg2_ref, bg_ref, tri_ref, s0_ref, of_ref, nw_ref, prev_ref,
         o_ref, sfin_ref, s_scr, oin_scr, q_scr, k_scr, b_scr, v_scr) = refs
    else:
        (cols_ref, wg2_ref, bg_ref, tri_ref, s0_ref, prev_ref,
         o_ref, sfin_ref, s_scr, oin_scr, q_scr, k_scr, b_scr, v_scr) = refs
    del prev_ref
    tl = nc * CH
    c = pl.program_id(1)
    nt = pl.num_programs(1)
    tt = nt - 1 - c if reverse else c
    pos = tt % seg
    start = (pos == seg - 1) if reverse else (pos == 0)

    @pl.when(start)
    def _():
        s_scr[...] = s0_ref[0]

    def chunk(ref, j, width):
        return ref[0, :, j * width:(j + 1) * width] if colmajor else ref[0, j * CH:(j + 1) * CH, :]

    cols = jnp.concatenate([chunk(cols_ref, j, GLA_PCOLS) for j in range(nc)], axis=0)
    q = cols[:, 0:GLA_QW] * (GLA_DK ** -0.5)
    k = cols[:, GLA_QW:2 * GLA_QW]
    v = cols[:, 2 * GLA_QW:2 * GLA_QW + GLA_VW]
    gd = cols[:, 2 * GLA_QW + 2 * GLA_VW:]
    z = _mm(gd, wg2_ref[...]) + bg_ref[...]
    la = -_softplus(-z) * (1.0 / GLA_TAU)
    b = _mm_exact_lhs01(tri_ref[...], la)
    tots = [b[j * CH:j * CH + 1, :] if reverse else b[(j + 1) * CH - 1:(j + 1) * CH, :] for j in range(nc)]
    totb = jnp.concatenate([jnp.broadcast_to(t, (CH, GLA_QW)) for t in tots], axis=0)
    qt = q * jnp.exp(b)
    kh = k * jnp.exp(totb - b)

    ti = lax.broadcasted_iota(jnp.int32, (CH, 128), 0)
    si = lax.broadcasted_iota(jnp.int32, (CH, 128), 1) & (CH - 1)
    incl = (si >= ti) if reverse else (si <= ti)
    lane = lax.broadcasted_iota(jnp.int32, (GLA_VP, 128), 1)
    zv = jnp.zeros((CH, GLA_VP), F32)
    units = [(j, p) for j in range(nc) for p in range(GLA_NP)]

    def subk(x, u):
        return x[u[0] * CH:(u[0] + 1) * CH, u[1] * 128:(u[1] + 1) * 128]

    def subv(x, u):
        return x[u[0] * CH:(u[0] + 1) * CH, u[1] * 256:(u[1] + 1) * 256]

    def vbd(x):
        return jnp.concatenate([jnp.concatenate([x[:, 0:GLA_VP], zv], axis=1),
                                jnp.concatenate([zv, x[:, GLA_VP:2 * GLA_VP]], axis=1)], axis=0)

    safe = jnp.min(b) >= -GLA_SAFE

    @pl.when(safe)
    def _():
        kt = k * jnp.exp(-b)
        att = {u: jnp.where(incl, _mm(subk(qt, u), _bd(subk(kt, u)), _NT), 0.0) for u in units}
        for u in units:
            oin_scr[u[0] * CH:(u[0] + 1) * CH, u[1] * 256:(u[1] + 1) * 256] = _mm(att[u], vbd(subv(v, u)))

    @pl.when(jnp.logical_not(safe))
    def _():
        q_scr[...] = q
        k_scr[...] = k
        b_scr[...] = b
        v_scr[...] = v
        sidx = lax.broadcasted_iota(jnp.int32, (CH, 1), 0)

        def row(i, _):
            t = i % CH
            base = pl.multiple_of(i - t, CH)
            msk = (sidx >= t) if reverse else (sidx <= t)
            d = jnp.where(msk, b_scr[pl.ds(i, 1), :] - b_scr[pl.ds(base, CH), :], 0.0)
            wgt = jnp.where(msk, jnp.exp(d), 0.0) * k_scr[pl.ds(base, CH), :] * q_scr[pl.ds(i, 1), :]
            vj = v_scr[pl.ds(base, CH), :]
            parts = []
            for h in range(GLA_H):
                col = jnp.sum(wgt[:, h * GLA_KP:(h + 1) * GLA_KP], axis=1, keepdims=True)
                parts.append(jnp.sum(col * vj[:, h * GLA_VP:(h + 1) * GLA_VP], axis=0, keepdims=True))
            oin_scr[pl.ds(i, 1), :] = jnp.concatenate(parts, axis=1)
            return 0

        lax.fori_loop(0, tl, row, 0)

    psi = {}
    for u in units:
        rr = _mm(subv(v, u), subk(kh, u), _TN)
        psi[u] = jnp.where(lane < 64, rr[0:GLA_VP], rr[GLA_VP:2 * GLA_VP])
    st = [s_scr[p] for p in range(GLA_NP)]
    rows = [None] * nc
    for j in (range(nc - 1, -1, -1) if reverse else range(nc)):
        o_j = []
        for p in range(GLA_NP):
            u = (j, p)
            o_j.append(_mm(subk(qt, u), _bd(st[p]), _NT))
            st[p] = st[p] * jnp.exp(tots[j][:, p * 128:(p + 1) * 128]) + psi[u]
        rows[j] = jnp.concatenate(o_j, axis=1)
    for p in range(GLA_NP):
        s_scr[p] = st[p]
        sfin_ref[0, p] = st[p]
    o = jnp.concatenate(rows, axis=0) + oin_scr[...]

    if epilogue:
        of = jnp.concatenate([chunk(of_ref, j, GLA_VW) for j in range(nc)], axis=0)
        o = of + o
        g = cols[:, 2 * GLA_QW + GLA_VW:2 * GLA_QW + 2 * GLA_VW]
        parts = []
        for h in range(GLA_H):
            oh = o[:, h * GLA_VP:(h + 1) * GLA_VP]
            ms = jnp.sum(oh * oh, axis=-1, keepdims=True) * (1.0 / GLA_DV)
            parts.append(oh * lax.rsqrt(ms + NORM_EPS))
        o = jnp.concatenate(parts, axis=1) * nw_ref[...] * _silu(g)
    for j in range(nc):
        if colmajor:
            o_ref[0, :, j * GLA_VW:(j + 1) * GLA_VW] = o[j * CH:(j + 1) * CH]
        else:
            o_ref[0, j * CH:(j + 1) * CH, :] = o[j * CH:(j + 1) * CH]


def _gla_dir(reverse, colmajor, cols, prm, s0, of, prev_out, seg, slab0, nb):
    ns, sl, _ = cols.shape
    nc = GLA_NC
    tl = nc * CH
    nt = sl // tl
    d = 1 if reverse else 0
    epilogue = of is not None
    tidx = (lambda c: nt - 1 - c) if reverse else (lambda c: c)
    nreq = nb * (nt // seg)
    if colmajor:
        assert sl == GRID_W * CH
        view = lambda a: a.reshape(ns, CH, GRID_W * a.shape[-1])
        tok = lambda w: pl.BlockSpec((1, CH, nc * w), lambda s, c: (s + slab0, 0, tidx(c)))
    else:
        view = lambda a: a
        tok = lambda w: pl.BlockSpec((1, tl, w), lambda s, c: (s + slab0, tidx(c), 0))
    full = lambda a: pl.BlockSpec(a.shape, lambda s, c: (0,) * a.ndim)
    sspec = pl.BlockSpec((1, GLA_NP, GLA_VP, 128), lambda s, c: (s * (nt // seg) + tidx(c) // seg, 0, 0, 0))
    ins = [view(cols), prm['wg2'][d], prm['bg'][d], prm['tri'][d], s0]
    in_specs = [tok(GLA_PCOLS), full(ins[1]), full(ins[2]), full(ins[3]), sspec]
    if epilogue:
        ins += [view(of), prm['nw']]
        in_specs += [tok(GLA_VW), full(prm['nw'])]
    aliases = {}
    if prev_out is None:
        ins.append(jnp.zeros((8, 128), F32))
        in_specs.append(pl.BlockSpec((8, 128), lambda s, c: (0, 0)))
    else:
        ins.append(view(prev_out))
        in_specs.append(pl.BlockSpec(memory_space=pl.ANY))
        aliases = {len(ins) - 1: 0}
    o, sfin = pl.pallas_call(
        functools.partial(_gla_kernel, reverse, epilogue, colmajor, nc, seg),
        grid=(nb, nt), in_specs=in_specs,
        out_specs=[tok(GLA_VW), sspec],
        out_shape=[jax.ShapeDtypeStruct((ns, CH, GRID_W * GLA_VW) if colmajor else (ns, sl, GLA_VW), F32),
                   jax.ShapeDtypeStruct((nreq, GLA_NP, GLA_VP, 128), F32)],
        scratch_shapes=[pltpu.VMEM((GLA_NP, GLA_VP, 128), F32), pltpu.VMEM((tl, GLA_VW), F32),
                        pltpu.VMEM((tl, GLA_QW), F32), pltpu.VMEM((tl, GLA_QW), F32),
                        pltpu.VMEM((tl, GLA_QW), F32), pltpu.VMEM((tl, GLA_VW), F32)],
        input_output_aliases=aliases,
        compiler_params=_cp(("arbitrary", "arbitrary")),
    )(*ins)
    return o.reshape(ns, sl, GLA_VW), sfin


def _post_kernel(x_ref, ol_ref, or_ref, og_ref, m_ref, g_ref, w1_ref, w2_ref, w3_ref, wrt_ref,
                 x1_ref, h2_ref, pt_ref):
    att = (_dg(ol_ref[0].astype(BF16), w1_ref[...], _NN) + _dg(or_ref[0].astype(BF16), w2_ref[...], _NN)
           + _dg(og_ref[0].astype(BF16), w3_ref[...], _NN))
    x1 = x_ref[0] + m_ref[0, 2:3, :] * att
    x1_ref[0] = x1
    h2 = _rmsnorm_mod(x1, g_ref[...], m_ref[0, 4:5, :], m_ref[0, 3:4, :])
    h2_ref[0] = h2
    logits = lax.dot_general(wrt_ref[...], h2, _NT, precision=lax.Precision.HIGHEST,
                             preferred_element_type=F32)
    mx = jnp.max(logits, axis=0, keepdims=True)
    e = jnp.exp(logits - mx)
    pt_ref[0] = e / jnp.sum(e, axis=0, keepdims=True)


def _post(x, o_lru, o_rw, o_gla, mods, g, w1, w2, w3, wrt, tl=512):
    ns, sl, _ = x.shape
    tok = lambda c: pl.BlockSpec((1, tl, c), lambda s, t: (s, t, 0))
    full = lambda a: pl.BlockSpec(a.shape, lambda s, t: (0,) * a.ndim)
    return pl.pallas_call(
        _post_kernel,
        grid=(ns, sl // tl),
        in_specs=[tok(D), tok(D_LRU), tok(D_RW), tok(GLA_VW), pl.BlockSpec((1, 6, D), lambda s, t: (s, 0, 0)),
                  full(g), full(w1), full(w2), full(w3), full(wrt)],
        out_specs=[tok(D), tok(D), pl.BlockSpec((1, N_EXPERTS, tl), lambda s, t: (s, 0, t))],
        out_shape=[jax.ShapeDtypeStruct((ns, sl, D), F32), jax.ShapeDtypeStruct((ns, sl, D), F32),
                   jax.ShapeDtypeStruct((ns, N_EXPERTS, sl), F32)],
        compiler_params=_cp(("arbitrary", "arbitrary")),
    )(x, o_lru, o_rw, o_gla, mods, g, w1, w2, w3, wrt)


def _gather_kernel(cap, idx_ref, h_ref, o_ref, buf):
    s = pl.program_id(0)
    e = pl.program_id(1)
    base = (s * N_EXPERTS + e) * cap

    def body(j, _):
        row = idx_ref[base + j]
        buf[pl.ds(j, 1), :] = h_ref[0, pl.ds(row, 1), :]
        return 0

    lax.fori_loop(0, cap, body, 0, unroll=8)
    o_ref[0, 0] = buf[...].astype(BF16)


def _moe_gather(h2, idx_flat, cap):
    ns, sl, _ = h2.shape
    return pl.pallas_call(
        functools.partial(_gather_kernel, cap),
        grid_spec=pltpu.PrefetchScalarGridSpec(
            num_scalar_prefetch=1, grid=(ns, N_EXPERTS),
            in_specs=[pl.BlockSpec((1, sl, D), lambda s, e, idx: (s, 0, 0))],
            out_specs=pl.BlockSpec((1, 1, cap, D), lambda s, e, idx: (e, s, 0, 0)),
            scratch_shapes=[pltpu.VMEM((cap, D), F32)]),
        out_shape=jax.ShapeDtypeStruct((N_EXPERTS, ns, cap, D), BF16),
        compiler_params=_cp(("arbitrary", "arbitrary")),
    )(idx_flat, h2)


def _expert_kernel(x_ref, wg_ref, wu_ref, wd_ref, o_ref, wgb, wub, wdb):
    @pl.when(pl.program_id(1) == 0)
    def _():
        wgb[...] = wg_ref[0].astype(BF16)
        wub[...] = wu_ref[0].astype(BF16)
        wdb[...] = wd_ref[0].astype(BF16)

    x = x_ref[0, 0]
    hid = _silu(_dg(x, wgb[...], _NN)) * _dg(x, wub[...], _NN)
    o_ref[0, 0] = _dg(hid.astype(BF16), wdb[...], _NN)


def _moe_experts(xg, wg, wu, wd, l):
    ne, ns, cap, _ = xg.shape
    de = wg.shape[-1]
    wspec = lambda a: pl.BlockSpec((None, 1) + a.shape[2:], lambda e, t: (l, e, 0, 0))
    return pl.pallas_call(
        _expert_kernel,
        grid=(ne, ns),
        in_specs=[pl.BlockSpec((1, 1, cap, D), lambda e, t: (e, t, 0, 0)), wspec(wg), wspec(wu), wspec(wd)],
        out_specs=pl.BlockSpec((1, 1, cap, D), lambda e, t: (e, t, 0, 0)),
        out_shape=jax.ShapeDtypeStruct((ne, ns, cap, D), F32),
        scratch_shapes=[pltpu.VMEM((D, de), BF16), pltpu.VMEM((D, de), BF16), pltpu.VMEM((de, D), BF16)],
        compiler_params=_cp(("arbitrary", "arbitrary")),
    )(xg, wg, wu, wd)


def _scatter_kernel(cap, idx_ref, gate_ref, y_ref, o_ref):
    s = pl.program_id(0)
    e = pl.program_id(1)
    base = (s * N_EXPERTS + e) * cap

    @pl.when(e == 0)
    def _():
        o_ref[...] = jnp.zeros(o_ref.shape, F32)

    def body(j, _):
        row = idx_ref[base + j]
        gt = gate_ref[base + j]
        o_ref[0, pl.ds(row, 1), :] = o_ref[0, pl.ds(row, 1), :] + gt * y_ref[0, 0, pl.ds(j, 1), :]
        return 0

    lax.fori_loop(0, cap, body, 0, unroll=8)


def _moe_scatter(yo, idx_flat, gate_flat, sl):
    ne, ns, cap, _ = yo.shape
    return pl.pallas_call(
        functools.partial(_scatter_kernel, cap),
        grid_spec=pltpu.PrefetchScalarGridSpec(
            num_scalar_prefetch=2, grid=(ns, ne),
            in_specs=[pl.BlockSpec((1, 1, cap, D), lambda s, e, i, g: (e, s, 0, 0))],
            out_specs=pl.BlockSpec((1, sl, D), lambda s, e, i, g: (s, 0, 0))),
        out_shape=jax.ShapeDtypeStruct((ns, sl, D), F32),
        compiler_params=_cp(("arbitrary", "arbitrary")),
    )(idx_flat, gate_flat, yo)


def _cumsum_lanes(m, tri):
    wdt = tri.shape[0]
    carry = jnp.zeros((m.shape[0], 1), F32)
    outs = []
    for b0 in range(0, m.shape[1], wdt):
        c = _dg(m[:, b0:b0 + wdt].astype(BF16), tri, _NN) + carry
        outs.append(c)
        carry = c[:, wdt - 1:wdt]
    return outs[0] if len(outs) == 1 else jnp.concatenate(outs, axis=1)


def _route_kernel(cap, p_ref, tri_ref, idx_ref, gate_ref, cs_ref):
    x = p_ref[0]
    ne, n = x.shape
    bits = lax.bitcast_convert_type(x, jnp.int32)
    cur = jnp.zeros((ne, 1), jnp.int32)
    for b in range(30, -1, -1):
        t = cur | (1 << b)
        cnt = jnp.sum(jnp.where(bits >= t, 1.0, 0.0), axis=1, keepdims=True)
        cur = jnp.where(cnt >= cap, t, cur)
    gt = bits > cur
    need = cap - jnp.sum(jnp.where(gt, 1.0, 0.0), axis=1, keepdims=True)
    tri = tri_ref[...]
    eqf = jnp.where(bits == cur, 1.0, 0.0)
    rank_eq = _cumsum_lanes(eqf, tri)
    sel = jnp.where(gt, 1.0, jnp.where(rank_eq <= need, eqf, 0.0))
    cs_ref[...] = sel * _cumsum_lanes(sel, tri)

    tpos = lax.broadcasted_iota(jnp.int32, (1, n), 1)
    thi = (tpos >> 6).astype(F32)
    tlo = (tpos & 63).astype(F32)
    jcol = (lax.broadcasted_iota(jnp.int32, (cap, 1), 0) + 1).astype(F32)
    piece = min(n, 1024)

    def body(e, _):
        csr = cs_ref[pl.ds(e, 1), :]
        g = p_ref[0, pl.ds(e, 1), :]
        gh = g.astype(BF16).astype(F32)
        r1 = g - gh
        gm = r1.astype(BF16).astype(F32)
        gl = r1 - gm
        lmat = jnp.concatenate([thi, tlo, gh, gm, gl, jnp.zeros((3, n), F32)], axis=0).astype(BF16)
        acc = jnp.zeros((8, cap), F32)
        for c0 in range(0, n, piece):
            onehot = jnp.where(csr[:, c0:c0 + piece] == jcol, 1.0, 0.0).astype(BF16)
            acc = acc + _dg(lmat[:, c0:c0 + piece], onehot, _NT)
        idx_ref[0, pl.ds(e, 1), :] = (acc[0:1] * 64.0 + acc[1:2]).astype(jnp.int32)
        gate_ref[0, pl.ds(e, 1), :] = acc[2:3] + (acc[3:4] + acc[4:5])
        return 0

    lax.fori_loop(0, ne, body, 0)


def _route_call(probs, cap):
    nreq, ne, n = probs.shape
    assert n <= 4096
    wdt = min(n, 256)
    tri = jnp.asarray(np.triu(np.ones((wdt, wdt), np.float32)), BF16)
    return pl.pallas_call(
        functools.partial(_route_kernel, cap),
        grid=(nreq,),
        in_specs=[pl.BlockSpec((1, ne, n), lambda q: (q, 0, 0)), pl.BlockSpec((wdt, wdt), lambda q: (0, 0))],
        out_specs=[pl.BlockSpec((1, ne, cap), lambda q: (q, 0, 0)), pl.BlockSpec((1, ne, cap), lambda q: (q, 0, 0))],
        out_shape=[jax.ShapeDtypeStruct((nreq, ne, cap), jnp.int32), jax.ShapeDtypeStruct((nreq, ne, cap), F32)],
        scratch_shapes=[pltpu.VMEM((ne, n), F32)],
        compiler_params=_cp(("arbitrary",)),
    )(probs, tri)


def _route(probs_t, p_b, p_l):
    ns, ne, sl = probs_t.shape
    cap_p = 2 * p_l // ne
    pp = jnp.swapaxes(probs_t[0].reshape(ne, p_b, p_l), 0, 1)
    ip, gp = _route_call(pp, cap_p)
    ip = ip + (jnp.arange(p_b, dtype=jnp.int32) * p_l)[:, None, None]
    ip = jnp.swapaxes(ip, 0, 1).reshape(1, ne, p_b * cap_p)
    gp = jnp.swapaxes(gp, 0, 1).reshape(1, ne, p_b * cap_p)
    cap = 2 * sl // ne
    is_, gs = _route_call(probs_t[1:], cap)
    idx = jnp.concatenate([ip, is_], axis=0)
    gates = jnp.concatenate([gp, gs], axis=0)
    return idx.reshape(-1), gates.reshape(-1), cap


def _final_kernel(x_ref, moe_ref, m_ref, g_ref, o_ref):
    x = x_ref[0] + m_ref[0, 5:6, :] * moe_ref[0]
    o_ref[0] = (x * lax.rsqrt(jnp.mean(x * x, axis=-1, keepdims=True) + NORM_EPS)) * g_ref[...]


def _final(x, moe, mods, g, slab0, nb, tl=512):
    ns, sl, _ = x.shape
    tok = pl.BlockSpec((1, tl, D), lambda s, t: (s + slab0, t, 0))
    return pl.pallas_call(
        _final_kernel,
        grid=(nb, sl // tl),
        in_specs=[tok, tok, pl.BlockSpec((1, 6, D), lambda s, t: (s + slab0, 0, 0)),
                  pl.BlockSpec(g.shape, lambda s, t: (0, 0))],
        out_specs=pl.BlockSpec((1, tl, D), lambda s, t: (s, t, 0)),
        out_shape=jax.ShapeDtypeStruct((nb, sl, D), F32),
        compiler_params=_cp(("arbitrary", "arbitrary")),
    )(x, moe, mods, g)


def _gla_col_src():
    src = np.full((GLA_PCOLS,), -1, np.int64)
    for h in range(GLA_H):
        for j in range(GLA_DK):
            src[h * GLA_KP + j] = h * GLA_DK + j
            src[GLA_QW + h * GLA_KP + j] = GLA_H * GLA_DK + h * GLA_DK + j
        for j in range(GLA_DV):
            src[2 * GLA_QW + h * GLA_VP + j] = 2 * GLA_H * GLA_DK + h * GLA_DV + j
            src[2 * GLA_QW + GLA_VW + h * GLA_VP + j] = 2 * GLA_H * GLA_DK + D_GLA + h * GLA_DV + j
    for j in range(2 * GLA_R):
        src[2 * GLA_QW + 2 * GLA_VW + j] = 2 * GLA_H * GLA_DK + 2 * D_GLA + j
    return src


_GLA_SRC = _gla_col_src()


def _pad_cols(a, src):
    valid = jnp.asarray(src >= 0)
    return jnp.where(valid, jnp.take(a, jnp.asarray(np.maximum(src, 0)), axis=-1), 0.0)


def _key_src():
    src = np.full((GLA_QW,), -1, np.int64)
    for h in range(GLA_H):
        src[h * GLA_KP:h * GLA_KP + GLA_DK] = np.arange(h * GLA_DK, (h + 1) * GLA_DK)
    return src


def _val_src():
    src = np.full((GLA_VW,), -1, np.int64)
    for h in range(GLA_H):
        src[h * GLA_VP:h * GLA_VP + GLA_DV] = np.arange(h * GLA_DV, (h + 1) * GLA_DV)
    return src


_KEY_SRC = _key_src()
_VAL_SRC = _val_src()


def _tri_consts():
    t = np.arange(CH)
    lower = (t[None, :] <= t[:, None]).astype(np.float32)
    return jnp.asarray(lower, BF16), jnp.asarray(lower.T, BF16)


def _layer_params(l, a):
    p = {}
    w_in = a['w_in'][l]
    p['w_lru'] = w_in[:, :2 * D_LRU].astype(BF16)
    p['w_rw'] = w_in[:, 2 * D_LRU:2 * D_LRU + RW_COLS].astype(BF16)
    p['w_gla'] = _pad_cols(w_in[:, 2 * D_LRU + RW_COLS:], _GLA_SRC).astype(BF16)
    blk = jnp.asarray(np.kron(np.eye(LRU_BLOCKS, dtype=np.float32), np.ones((LRU_BW, LRU_BW), np.float32)))

    def dense(wb):
        rows = jnp.concatenate([jnp.tile(wb[n], (1, LRU_BLOCKS)) for n in range(LRU_BLOCKS)], axis=0)
        return rows * blk

    p['lru_wblk'] = jnp.concatenate([dense(a['lru_wa'][l, 0]), dense(a['lru_wx'][l, 0]),
                                     dense(a['lru_wa'][l, 1]), dense(a['lru_wx'][l, 1])], axis=1).astype(BF16)
    p['lru_bias'] = jnp.concatenate([a['lru_ba'][l, 0], a['lru_bx'][l, 0], a['lru_ba'][l, 1], a['lru_bx'][l, 1]])[None, :]
    p['lru_cw'] = a['lru_conv_w'][l]
    p['lru_cb'] = a['lru_conv_b'][l][None, :]
    p['lru_lam'] = a['lru_lambda'][l]
    lower, upper = _tri_consts()
    hones = jnp.asarray(np.kron(np.eye(RW_H, dtype=np.float32), np.ones((RW_HD, RW_HD), np.float32)), BF16)

    def dirpad(w, d):
        z = jnp.zeros_like(w)
        return jnp.concatenate([w, z] if d == 0 else [z, w], axis=0)

    p['rw'] = {
        'mu': a['rw_mu'][l],
        'w0': [a['rw_w0'][l, d][None, :] for d in range(2)],
        'w2': [dirpad(a['rw_w2'][l, d], d) for d in range(2)],
        'a0': [a['rw_a0'][l, d][None, :] for d in range(2)],
        'a2': [dirpad(a['rw_a2'][l, d], d) for d in range(2)],
        'g2': a['rw_g2'][l],
        'kkw': a['rw_kk'][l][None, :],
        'ka': a['rw_ka'][l][None, :],
        'rk': a['rw_rk'][l].reshape(2, D_RW),
        'lnw': a['rw_ln_w'][l][None, :],
        'lnb': a['rw_ln_b'][l][None, :],
        'hones': hones,
        'tri': [jnp.kron(jnp.eye(RW_NC, dtype=BF16), lower), jnp.kron(jnp.eye(RW_NC, dtype=BF16), upper)],
    }

    def gpad(w, d):
        wp = _pad_cols(w, _KEY_SRC)
        out = jnp.zeros((128, GLA_QW), F32)
        return lax.dynamic_update_slice(out, wp, (d * GLA_R, 0))

    p['gla'] = {
        'wg2': [gpad(a['gla_wg2'][l, d], d) for d in range(2)],
        'bg': [_pad_cols(a['gla_bg'][l, d][None, :], _KEY_SRC) for d in range(2)],
        'nw': _pad_cols(a['gla_norm_w'][l].reshape(1, D_GLA), _VAL_SRC),
        'tri': [jnp.kron(jnp.eye(GLA_NC, dtype=BF16), lower), jnp.kron(jnp.eye(GLA_NC, dtype=BF16), upper)],
    }
    w_out = a['w_out'][l]
    p['wo1'] = w_out[:D_LRU].astype(BF16)
    p['wo2'] = w_out[D_LRU:D_LRU + D_RW].astype(BF16)
    p['wo3'] = _pad_cols(w_out[D_LRU + D_RW:].T, _VAL_SRC).T.astype(BF16)
    p['wrt'] = a['w_router'][l].T
    p['norm1'] = a['norm1_g'][l][None, :]
    p['norm2'] = a['norm2_g'][l][None, :]
    return p


def _gla_state_in(sg):
    st = jnp.swapaxes(sg, -1, -2)
    st = jnp.pad(st, ((0, 0), (0, 0), (0, GLA_VP - GLA_DV), (0, GLA_KP - GLA_DK)))
    n = st.shape[0]
    return st.reshape(n, GLA_NP, 2, GLA_VP, GLA_KP).transpose(0, 1, 3, 2, 4).reshape(n, GLA_NP, GLA_VP, 2 * GLA_KP)


def _gla_state_out(sp):
    n = sp.shape[0]
    st = sp.reshape(n, GLA_NP, GLA_VP, 2, GLA_KP).transpose(0, 1, 3, 2, 4).reshape(n, GLA_H, GLA_VP, GLA_KP)
    return jnp.swapaxes(st[:, :, :GLA_DV, :GLA_DK], -1, -2)


def _forward(a):
    x_prompt, x_sample = a['x_prompt'], a['x_sample']
    p_b, p_l, _ = x_prompt.shape
    s_b, sl, _ = x_sample.shape
    assert p_b * p_l == sl
    ns = s_b + 1
    nreq = p_b + s_b

    cmat = jnp.zeros((16, D), F32).at[:s_b].set(a['c']).at[s_b].set(a['c_ctx'])
    mods_all = _mods(cmat, a['w_ada'], a['b_ada'])
    order = np.array([s_b] + list(range(s_b)))
    mods = mods_all[:, order].reshape(DEPTH, ns, 6, D)

    x = jnp.concatenate([x_prompt.reshape(1, sl, D), x_sample], axis=0)
    moe = None
    st_lru, st_rw, st_gla = [], [], []
    for l in range(DEPTH):
        p = _layer_params(l, a)
        x, lru_in, rw_cols, gla_cols = _pre(x, moe, mods[l - 1] if l else None, mods[l], p['norm1'],
                                            p['w_lru'], p['w_rw'], p['w_gla'])
        h0 = jnp.concatenate([jnp.zeros((1, 2, D_LRU), F32), a['state_lru'][:, l]], axis=0)
        o_lru, hfin = _lru(lru_in, p['lru_cw'], p['lru_cb'], p['lru_wblk'], p['lru_bias'], p['lru_lam'], h0, p_l, p_b)
        st_lru.append(jnp.swapaxes(hfin[0], 0, 1))
        zs = jnp.zeros((1, RW_H, RW_HD, RW_HD), F32)
        s0f = _rw_pack(jnp.concatenate([zs, a['state_rwkv'][:, l, 0]], axis=0))
        s0b = _rw_pack(jnp.concatenate([zs, a['state_rwkv'][:, l, 1]], axis=0))
        of, sf = _rwkv_dir(False, rw_cols, p['rw'], s0f, None, p_l, nreq)
        o_rw, sb = _rwkv_dir(True, rw_cols, p['rw'], s0b, of, p_l, nreq)
        st_rw.append(jnp.stack([_rw_unpack(sf[:p_b]), _rw_unpack(sb[:p_b])], axis=1))
        zg = jnp.zeros((p_b, GLA_NP, GLA_VP, 2 * GLA_KP), F32)
        gp_f, gsf = _gla_dir(False, False, gla_cols, p['gla'], zg, None, None, p_l // (GLA_NC * CH), 0, 1)
        gs_f, _ = _gla_dir(False, True, gla_cols, p['gla'], _gla_state_in(a['state_gla'][:, l, 0]), None, gp_f,
                           sl // (GLA_NC * CH), 1, s_b)
        gp, gsb = _gla_dir(True, False, gla_cols, p['gla'], zg, gs_f, None, p_l // (GLA_NC * CH), 0, 1)
        o_gla, _ = _gla_dir(True, True, gla_cols, p['gla'], _gla_state_in(a['state_gla'][:, l, 1]), gs_f, gp,
                            sl // (GLA_NC * CH), 1, s_b)
        st_gla.append(jnp.stack([_gla_state_out(gsf), _gla_state_out(gsb)], axis=1))
        x, h2, probs_t = _post(x, o_lru, o_rw, o_gla, mods[l], p['norm2'], p['wo1'], p['wo2'], p['wo3'], p['wrt'])
        idx, gates, cap = _route(probs_t, p_b, p_l)
        xg = _moe_gather(h2, idx, cap)
        yo = _moe_experts(xg, a['w_gate'], a['w_up'], a['w_down'], l)
        moe = _moe_scatter(yo, idx, gates, sl)
    fg = a['final_norm_g'][None, :]
    y_prompt = _final(x, moe, mods[DEPTH - 1], fg, 0, 1).reshape(p_b, p_l, D)
    y_sample = _final(x, moe, mods[DEPTH - 1], fg, 1, s_b)
    return (y_prompt, y_sample, jnp.stack(st_lru, axis=1), jnp.stack(st_rw, axis=1), jnp.stack(st_gla, axis=1))


def kernel(x_prompt, x_sample, c, state_lru, state_rwkv, state_gla, c_ctx, norm1_g, norm2_g, w_ada, b_ada, w_in, lru_conv_w, lru_conv_b, lru_wa, lru_ba, lru_wx, lru_bx, lru_lambda, rw_mu, rw_w0, rw_w2, rw_a0, rw_a2, rw_g2, rw_kk, rw_ka, rw_rk, rw_ln_w, rw_ln_b, gla_wg2, gla_bg, gla_norm_w, w_out, w_router, w_gate, w_up, w_down, final_norm_g):
    return _forward(dict(locals()))
```

```python
import functools

import numpy as np
import jax
import jax.numpy as jnp
from jax import lax
from jax.experimental import pallas as pl
from jax.experimental.pallas import tpu as pltpu

F32 = jnp.float32
BF16 = jnp.bfloat16

D = 1024
DEPTH = 2
NORM_EPS = 1e-6
D_LRU = 256
LRU_BLOCKS = 4
LRU_BW = 64
LRU_C = 8.0
D_RW = 384
RW_H = 6
RW_HD = 64
RW_COLS = 1536
RW_GN_EPS = 64e-5
D_GLA = 384
GLA_H = 4
GLA_DV = 96
GLA_DK = 48
GLA_R = 16
GLA_TAU = 16.0
N_EXPERTS = 16
GRID_W = 64

GLA_KP = 64
GLA_VP = 128
GLA_QW = GLA_H * GLA_KP
GLA_VW = GLA_H * GLA_VP
GLA_PCOLS = 2 * GLA_QW + 2 * GLA_VW + 128

CH = 64
VMEM_LIMIT = 56 * 1024 * 1024


def _cp(sem, vmem=VMEM_LIMIT):
    return pltpu.CompilerParams(dimension_semantics=sem, vmem_limit_bytes=vmem)


def _sigmoid(x):
    return 1.0 / (1.0 + jnp.exp(-x))


def _softplus(x):
    return jnp.maximum(x, 0.0) + jnp.log1p(jnp.exp(-jnp.abs(x)))


def _one_minus_exp(y):
    e = jnp.exp(y)
    em1 = e - 1.0
    small = jnp.where(em1 == 0.0, y, em1 * y / jnp.log(e))
    return -jnp.where(y < -0.5, em1, small)


def _silu(x):
    return x * _sigmoid(x)


def _gelu_tanh(x):
    return 0.5 * x * (1.0 + jnp.tanh(0.7978845608028654 * (x + 0.044715 * (x * x * x))))


_NN = (((1,), (0,)), ((), ()))
_NT = (((1,), (1,)), ((), ()))
_TN = (((0,), (0,)), ((), ()))


def _dg(a, b, dims):
    return lax.dot_general(a, b, dims, preferred_element_type=F32)


def _split2(x):
    hi = x.astype(BF16)
    lo = (x - hi.astype(F32)).astype(BF16)
    return hi, lo


def _split3(x):
    hi = x.astype(BF16)
    r = x - hi.astype(F32)
    mid = r.astype(BF16)
    lo = (r - mid.astype(F32)).astype(BF16)
    return hi, mid, lo


def _bd(x):
    lane = lax.broadcasted_iota(jnp.int32, x.shape, 1) & 127
    zero = jnp.zeros_like(x)
    return jnp.concatenate([jnp.where(lane < 64, x, zero), jnp.where(lane >= 64, x, zero)], axis=0)


def _mm(a, b, dims=_NN, passes=1, bd=False):
    expand = _bd if bd else (lambda t: t)
    if passes == 1:
        return _dg(a.astype(BF16), expand(b.astype(BF16)), dims)
    ah, al = _split2(a)
    bh, bl = _split2(b)
    bh, bl = expand(bh), expand(bl)
    ca, cb = dims[0][0][0], dims[0][1][0]
    return _dg(jnp.concatenate([ah, al, ah], axis=ca), jnp.concatenate([bh, bh, bl], axis=cb), dims)


def _mm_exact_lhs01(m01, x):
    hi, mid, lo = _split3(x)
    return _dg(m01, hi, _NN) + (_dg(m01, mid, _NN) + _dg(m01, lo, _NN))


def _mm_exact_rhs01(x, m01):
    hi, mid, lo = _split3(x)
    return _dg(hi, m01, _NN) + (_dg(mid, m01, _NN) + _dg(lo, m01, _NN))


def _rmsnorm_mod(x, g, scale, shift):
    y = x * lax.rsqrt(jnp.mean(x * x, axis=-1, keepdims=True) + NORM_EPS)
    return (y * g) * (1.0 + scale) + shift


def _mods_kernel(c_ref, w_ref, b_ref, o_ref):
    c = c_ref[...]
    o_ref[0] = _mm(_silu(c), w_ref[0]) + b_ref[0]


def _mods(cmat, w_ada, b_ada):
    nt = 4
    tn = 6 * D // nt
    return pl.pallas_call(
        _mods_kernel,
        grid=(DEPTH, nt),
        in_specs=[pl.BlockSpec((16, D), lambda l, j: (0, 0)),
                  pl.BlockSpec((1, D, tn), lambda l, j: (l, 0, j)),
                  pl.BlockSpec((1, 1, tn), lambda l, j: (l, 0, j))],
        out_specs=pl.BlockSpec((1, 16, tn), lambda l, j: (l, 0, j)),
        out_shape=jax.ShapeDtypeStruct((DEPTH, 16, 6 * D), F32),
        compiler_params=_cp(("arbitrary", "arbitrary")),
    )(cmat, w_ada, b_ada.reshape(DEPTH, 1, 6 * D))


def _pre_kernel(has_res, *refs):
    if has_res:
        (x_ref, moe_ref, pm_ref, m_ref, g_ref, wl_ref, wr_ref, wg_ref,
         xo_ref, lru_ref, rw_ref, gla_ref) = refs
        x = x_ref[0] + pm_ref[0, 5:6, :] * moe_ref[0]
        xo_ref[0] = x
    else:
        x_ref, m_ref, g_ref, wl_ref, wr_ref, wg_ref, lru_ref, rw_ref, gla_ref = refs
        x = x_ref[0]
    h = _rmsnorm_mod(x, g_ref[...], m_ref[0, 1:2, :], m_ref[0, 0:1, :]).astype(BF16)
    lru_ref[0] = _dg(h, wl_ref[...], _NN)
    rw_ref[0] = _dg(h, wr_ref[...], _NN)
    gla_ref[0] = _dg(h, wg_ref[...], _NN)


def _pre(x, moe, prev_mods, mods, g, wl, wr, wg, tl=512):
    ns, sl, _ = x.shape
    has_res = moe is not None
    tok = lambda c: pl.BlockSpec((1, tl, c), lambda s, t: (s, t, 0))
    mspec = pl.BlockSpec((1, 6, D), lambda s, t: (s, 0, 0))
    full = lambda a: pl.BlockSpec(a.shape, lambda s, t: (0,) * a.ndim)
    ins = [x] + ([moe, prev_mods] if has_res else []) + [mods, g, wl, wr, wg]
    in_specs = [tok(D)] + ([tok(D), mspec] if has_res else []) + [mspec, full(g), full(wl), full(wr), full(wg)]
    outs = [jax.ShapeDtypeStruct((ns, sl, c), F32) for c in (2 * D_LRU, RW_COLS, GLA_PCOLS)]
    out_specs = [tok(2 * D_LRU), tok(RW_COLS), tok(GLA_PCOLS)]
    if has_res:
        outs = [jax.ShapeDtypeStruct((ns, sl, D), F32)] + outs
        out_specs = [tok(D)] + out_specs
    res = pl.pallas_call(
        functools.partial(_pre_kernel, has_res),
        grid=(ns, sl // tl), in_specs=in_specs, out_specs=out_specs, out_shape=outs,
        compiler_params=_cp(("arbitrary", "arbitrary")),
    )(*ins)
    if has_res:
        return res
    return [x] + list(res)


def _lru_kernel(tl, seg0, segn, in_ref, cw_ref, cb_ref, wb_ref, bias_ref, lam_ref, h0_ref,
                o_ref, hfin_ref, xp_ref, hf_ref):
    s = pl.program_id(0)
    sl = in_ref.shape[1]
    nt = sl // tl
    seg = jnp.where(s == 0, seg0, segn)
    c = D_LRU

    xp_ref[0:8, :] = jnp.zeros((8, c), F32)
    xp_ref[sl + 8:sl + 16, :] = jnp.zeros((8, c), F32)
    xp_ref[8:sl + 8, :] = in_ref[0, :, 0:c]
    hfin_ref[...] = jnp.zeros(hfin_ref.shape, F32)

    cw = cw_ref[...]
    cb = cb_ref[...]
    sp = _softplus(-lam_ref[...])
    rid = lax.broadcasted_iota(jnp.int32, (tl, 1), 0)
    wid = lax.broadcasted_iota(jnp.int32, (tl + 16, 1), 0)

    def gates(i, d):
        t0 = pl.multiple_of(i * tl, tl)
        pos = i % seg
        win = xp_ref[pl.ds(t0, tl + 16), :]
        lo = jnp.where(pos == 0, 8, 0)
        hi = jnp.where(pos == seg - 1, tl + 8, tl + 16)
        win = jnp.where(wid >= lo, jnp.where(wid < hi, win, 0.0), 0.0)
        xc = cb + cw[0:1] * win[6:6 + tl] + cw[1:2] * win[7:7 + tl] + cw[2:3] * win[8:8 + tl] + cw[3:4] * win[9:9 + tl]
        gt = _dg(xc.astype(BF16), wb_ref[:, 2 * c * d:2 * c * (d + 1)], _NN) + bias_ref[:, 2 * c * d:2 * c * (d + 1)]
        r = _sigmoid(gt[:, 0:c])
        ig = _sigmoid(gt[:, c:2 * c])
        log_a = (-LRU_C) * r * sp[d:d + 1]
        a = jnp.exp(log_a)
        u = jnp.sqrt(_one_minus_exp(2.0 * log_a)) * (ig * xc)
        return t0, pos, a, u

    def scan(a, u, rev):
        k = 1
        while k < tl:
            if rev:
                ok = rid < tl - k
                a_s = jnp.where(ok, pltpu.roll(a, tl - k, 0), 1.0)
                u_s = jnp.where(ok, pltpu.roll(u, tl - k, 0), 0.0)
            else:
                ok = rid >= k
                a_s = jnp.where(ok, pltpu.roll(a, k, 0), 1.0)
                u_s = jnp.where(ok, pltpu.roll(u, k, 0), 0.0)
            u = a * u_s + u
            a = a * a_s
            k *= 2
        return a, u

    def fwd(i, carry):
        t0, pos, a, u = gates(i, 0)
        a, u = scan(a, u, False)
        carry = jnp.where(pos == 0, h0_ref[0, 0:1, :], carry)
        h = a * carry + u
        hf_ref[pl.ds(t0, tl), :] = h
        last = h[tl - 1:tl, :]

        @pl.when(pos == seg - 1)
        def _():
            hfin_ref[0, 0, pl.ds(i // seg, 1), :] = last
        return last

    lax.fori_loop(0, nt, fwd, jnp.zeros((1, c), F32))

    def bwd(j, carry):
        i = nt - 1 - j
        t0, pos, a, u = gates(i, 1)
        a, u = scan(a, u, True)
        carry = jnp.where(pos == seg - 1, h0_ref[0, 1:2, :], carry)
        h = a * carry + u
        y = in_ref[0, pl.ds(t0, tl), c:2 * c]
        o_ref[0, pl.ds(t0, tl), :] = (hf_ref[pl.ds(t0, tl), :] + h) * _gelu_tanh(y)
        first = h[0:1, :]

        @pl.when(pos == 0)
        def _():
            hfin_ref[0, 1, pl.ds(i // seg, 1), :] = first
        return first

    lax.fori_loop(0, nt, bwd, jnp.zeros((1, c), F32))


def _lru(lru_in, cw, cb, wblk, bias, lam, h0, p_l, nreq0, tl=256):
    ns, sl, _ = lru_in.shape
    full = lambda a: pl.BlockSpec(a.shape, lambda s: (0,) * a.ndim)
    return pl.pallas_call(
        functools.partial(_lru_kernel, tl, p_l // tl, sl // tl),
        grid=(ns,),
        in_specs=[pl.BlockSpec((1, sl, 2 * D_LRU), lambda s: (s, 0, 0)),
                  full(cw), full(cb), full(wblk), full(bias), full(lam),
                  pl.BlockSpec((1, 2, D_LRU), lambda s: (s, 0, 0))],
        out_specs=[pl.BlockSpec((1, sl, D_LRU), lambda s: (s, 0, 0)),
                   pl.BlockSpec((1, 2, nreq0, D_LRU), lambda s: (s, 0, 0, 0))],
        out_shape=[jax.ShapeDtypeStruct((ns, sl, D_LRU), F32),
                   jax.ShapeDtypeStruct((ns, 2, nreq0, D_LRU), F32)],
        scratch_shapes=[pltpu.VMEM((sl + 16, D_LRU), F32), pltpu.VMEM((sl, D_LRU), F32)],
        compiler_params=_cp(("arbitrary",)),
    )(lru_in, cw, cb, wblk, bias, lam, h0)


RW_NC = 4
RW_P_INV = 3
RW_P_STATE = 3
RW_P_ATT = 1
RW_NP = RW_H // 2


def _rwkv_kernel(reverse, epilogue, nc, seg0, segn, *refs):
    if epilogue:
        (cols_ref, prev_ref, next_ref, mu_ref, w0_ref, w2_ref, a0_ref, a2_ref, g2_ref, kkw_ref, ka_ref,
         hones_ref, tri_ref, s0_ref, of_ref, a0o_ref, a2o_ref, rk_ref, lnw_ref, lnb_ref,
         o_ref, sfin_ref, s_scr) = refs
    else:
        (cols_ref, prev_ref, next_ref, mu_ref, w0_ref, w2_ref, a0_ref, a2_ref, g2_ref, kkw_ref, ka_ref,
         hones_ref, tri_ref, s0_ref, o_ref, sfin_ref, s_scr) = refs
    tl = nc * CH
    s = pl.program_id(0)
    c = pl.program_id(1)
    nt = pl.num_programs(1)
    tt = nt - 1 - c if reverse else c
    seg = jnp.where(s == 0, seg0, segn)
    pos = tt % seg
    first = pos == 0
    last = pos == seg - 1
    start = last if reverse else first

    @pl.when(start)
    def _():
        s_scr[...] = s0_ref[0]

    cols = cols_ref[0]
    prow = jnp.where(first, 0.0, prev_ref[0, 7:8, :])
    nrow = jnp.where(last, 0.0, next_ref[0, 0:1, :])
    rid = lax.broadcasted_iota(jnp.int32, (tl, 1), 0)
    prev = jnp.where(rid == 0, prow, pltpu.roll(cols, 1, 0))
    nxt = jnp.where(rid == tl - 1, nrow, pltpu.roll(cols, tl - 1, 0))
    xs = cols + mu_ref[0:1, :] * (prev - cols) + mu_ref[1:2, :] * (nxt - cols)

    w = D_RW
    r = xs[:, 0:w]
    k = xs[:, w:2 * w]
    v = xs[:, 2 * w:3 * w]
    wd = xs[:, 3 * w:3 * w + 128]
    ad = xs[:, 3 * w + 128:3 * w + 256]
    gd = xs[:, 3 * w + 256:3 * w + 384]

    hones = hones_ref[...]
    wlog = -_softplus(-(w0_ref[...] + _mm(jnp.tanh(wd), w2_ref[...]))) - 0.5
    lw = -jnp.exp(wlog)
    alpha = _sigmoid(a0_ref[...] + _mm(ad, a2_ref[...]))
    kk = k * kkw_ref[...]
    kk = kk * lax.rsqrt(_mm_exact_rhs01(kk * kk, hones) + 1e-12)
    kd = k * (1.0 + (alpha - 1.0) * ka_ref[...])

    bw = _mm_exact_lhs01(tri_ref[...], lw)
    tots = [bw[j * CH:j * CH + 1, :] if reverse else bw[(j + 1) * CH - 1:(j + 1) * CH, :] for j in range(nc)]
    totb = jnp.concatenate([jnp.broadcast_to(t, (CH, w)) for t in tots], axis=0)
    kal = kk * alpha
    rt = r * jnp.exp(bw)
    at = -kk * jnp.exp(bw - lw)
    en = jnp.exp(-bw)
    bt = kal * en
    kt = kd * en
    ec = jnp.exp(totb - bw)
    bh = kal * ec
    kh = kd * ec

    ti = lax.broadcasted_iota(jnp.int32, (CH, 128), 0)
    si = lax.broadcasted_iota(jnp.int32, (CH, 128), 1) & (CH - 1)
    strict = (si > ti) if reverse else (si < ti)
    incl = (si >= ti) if reverse else (si <= ti)
    eye2 = jnp.where(si == ti, 1.0, 0.0)
    lane = lax.broadcasted_iota(jnp.int32, (CH, 128), 1)
    ri = lax.broadcasted_iota(jnp.int32, (128, 128), 0)
    ci = lax.broadcasted_iota(jnp.int32, (128, 128), 1)
    same_head = (ri < 64) == (ci < 64)
    diag = ri == ci

    units = [(j, p) for j in range(nc) for p in range(RW_NP)]

    def sub(x, u):
        j, p = u
        return x[j * CH:(j + 1) * CH, p * 128:(p + 1) * 128]

    n_, rb_, ak_, rk_ = {}, {}, {}, {}
    for u in units:
        lhs = jnp.concatenate([sub(at, u), sub(rt, u)], axis=0)
        gb = _mm(lhs, sub(bt, u), _NT, RW_P_INV, bd=True)
        gk = _mm(lhs, sub(kt, u), _NT, RW_P_ATT, bd=True)
        n_[u] = jnp.where(strict, gb[0:CH], 0.0)
        rb_[u] = jnp.where(incl, gb[CH:2 * CH], 0.0)
        ak_[u] = jnp.where(strict, gk[0:CH], 0.0)
        rk_[u] = jnp.where(incl, gk[CH:2 * CH], 0.0)
    av_ = {u: _mm(ak_[u], sub(v, u), _NN, RW_P_ATT, bd=True) for u in units}
    tp_ = {u: eye2 + n_[u] for u in units}
    np_ = {u: _mm(n_[u], n_[u], _NN, RW_P_INV, bd=True) for u in units}
    p2 = 2
    while 2 * p2 < CH:
        both = {u: _mm(jnp.concatenate([np_[u], tp_[u]], axis=0), np_[u], _NN, RW_P_INV, bd=True) for u in units}
        tp_ = {u: tp_[u] + both[u][CH:2 * CH] for u in units}
        np_ = {u: both[u][0:CH] for u in units}
        p2 *= 2
    tp_ = {u: tp_[u] + _mm(tp_[u], np_[u], _NN, RW_P_INV, bd=True) for u in units}
    pq_ = {u: _mm(tp_[u], jnp.concatenate([sub(at, u), av_[u]], axis=1), _NN, RW_P_INV, bd=True) for u in units}
    om_, oi_, phi_, psi_ = {}, {}, {}, {}
    for u in units:
        pp, qq = pq_[u][:, 0:128], pq_[u][:, 128:256]
        x2 = _mm(rb_[u], pq_[u], _NN, RW_P_ATT, bd=True)
        om_[u] = sub(rt, u) + x2[:, 0:128]
        oi_[u] = x2[:, 128:256] + _mm(rk_[u], sub(v, u), _NN, RW_P_ATT, bd=True)
        wc = jnp.exp(tots[u[0]][:, u[1] * 128:(u[1] + 1) * 128])
        phi_[u] = jnp.where(same_head, _mm(pp, sub(bh, u), _TN, RW_P_INV), 0.0) + jnp.where(diag, wc, 0.0)
        rr = _mm(qq, sub(bh, u), _TN, RW_P_INV) + _mm(sub(v, u), sub(kh, u), _TN, RW_P_ATT)
        psi_[u] = jnp.where(lane < 64, rr[0:CH], rr[CH:2 * CH])

    st = [s_scr[p] for p in range(RW_NP)]
    rows = [None] * nc
    for j in (range(nc - 1, -1, -1) if reverse else range(nc)):
        o_j = []
        for p in range(RW_NP):
            u = (j, p)
            o_j.append(_mm(om_[u], st[p], _NT, RW_P_ATT, bd=True) + oi_[u])
            st[p] = _mm(st[p], phi_[u], _NN, RW_P_STATE) + psi_[u]
        rows[j] = jnp.concatenate(o_j, axis=1)
    for p in range(RW_NP):
        s_scr[p] = st[p]
        sfin_ref[0, p] = st[p]
    o = jnp.concatenate(rows, axis=0)

    if epilogue:
        o = of_ref[0] + o
        inv = 1.0 / RW_HD
        mean = _mm_exact_rhs01(o, hones) * inv
        dlt = o - mean
        var = _mm_exact_rhs01(dlt * dlt, hones) * inv
        on = dlt * lax.rsqrt(var + RW_GN_EPS) * lnw_ref[...] + lnb_ref[...]
        alpha_o = _sigmoid(a0o_ref[...] + _mm(ad, a2o_ref[...]))
        kd_o = k * (1.0 + (alpha_o - 1.0) * ka_ref[...])
        d_here, d_oth = (1, 0) if reverse else (0, 1)
        rk = rk_ref[...]
        bonus = _mm_exact_rhs01(r * (kd * rk[d_here:d_here + 1] + kd_o * rk[d_oth:d_oth + 1]), hones) * v
        g = _mm(_sigmoid(gd), g2_ref[...])
        o = (on + bonus) * g
    o_ref[0] = o


def _rwkv_dir(reverse, cols, prm, s0, of, p_l, nreq):
    ns, sl, _ = cols.shape
    tl = RW_NC * CH
    nt = sl // tl
    d = 1 if reverse else 0
    epilogue = of is not None
    tidx = (lambda c: nt - 1 - c) if reverse else (lambda c: c)
    nb8 = sl // 8
    r8 = tl // 8
    full = lambda a: pl.BlockSpec(a.shape, lambda s, c: (0,) * a.ndim)
    seg0 = p_l // tl

    def req(s, c):
        return jnp.where(s == 0, tidx(c) // seg0, nreq - ns + s)

    ins = [cols, cols, cols, prm['mu'], prm['w0'][d], prm['w2'][d], prm['a0'][d], prm['a2'][d], prm['g2'],
           prm['kkw'], prm['ka'], prm['hones'], prm['tri'][d], s0]
    in_specs = [pl.BlockSpec((1, tl, RW_COLS), lambda s, c: (s, tidx(c), 0)),
                pl.BlockSpec((1, 8, RW_COLS), lambda s, c: (s, jnp.maximum(tidx(c) * r8 - 1, 0), 0)),
                pl.BlockSpec((1, 8, RW_COLS), lambda s, c: (s, jnp.minimum(tidx(c) * r8 + r8, nb8 - 1), 0))]
    in_specs += [full(a) for a in ins[3:13]]
    in_specs += [pl.BlockSpec((1, RW_NP, RW_HD, 128), lambda s, c: (s, 0, 0, 0))]
    if epilogue:
        ins += [of, prm['a0'][1 - d], prm['a2'][1 - d], prm['rk'], prm['lnw'], prm['lnb']]
        in_specs += [pl.BlockSpec((1, tl, D_RW), lambda s, c: (s, tidx(c), 0))]
        in_specs += [full(a) for a in ins[15:]]
    return pl.pallas_call(
        functools.partial(_rwkv_kernel, reverse, epilogue, RW_NC, seg0, nt),
        grid=(ns, nt), in_specs=in_specs,
        out_specs=[pl.BlockSpec((1, tl, D_RW), lambda s, c: (s, tidx(c), 0)),
                   pl.BlockSpec((1, RW_NP, RW_HD, 128), lambda s, c: (req(s, c), 0, 0, 0))],
        out_shape=[jax.ShapeDtypeStruct((ns, sl, D_RW), F32),
                   jax.ShapeDtypeStruct((nreq, RW_NP, RW_HD, 128), F32)],
        scratch_shapes=[pltpu.VMEM((RW_NP, RW_HD, 128), F32)],
        compiler_params=_cp(("arbitrary", "arbitrary")),
    )(*ins)


def _rw_pack(st):
    n = st.shape[0]
    return st.reshape(n, RW_NP, 2, RW_HD, RW_HD).transpose(0, 1, 3, 2, 4).reshape(n, RW_NP, RW_HD, 2 * RW_HD)


def _rw_unpack(sp):
    n = sp.shape[0]
    return sp.reshape(n, RW_NP, RW_HD, 2, RW_HD).transpose(0, 1, 3, 2, 4).reshape(n, RW_H, RW_HD, RW_HD)


GLA_NC = 4
GLA_NC_COL = 8
GLA_NP = GLA_H // 2
GLA_SAFE = 40.0


def _gla_kernel(reverse, epilogue, colmajor, nc, seg, *refs):
    if epilogue:
        (cols_ref, wg2_ref, bg_ref, tri_ref, s0_ref, of_ref, nw_ref, prev_ref,
         o_ref, sfin_ref, s_scr, oin_scr, q_scr, k_scr, b_scr, v_scr, cin_scr, io_scr) = refs
    else:
        (cols_ref, wg2_ref, bg_ref, tri_ref, s0_ref, prev_ref,
         o_ref, sfin_ref, s_scr, oin_scr, q_scr, k_scr, b_scr, v_scr, cin_scr, io_scr) = refs
    del prev_ref
    tl = nc * CH
    c = pl.program_id(1)
    nt = pl.num_programs(1)
    tt = nt - 1 - c if reverse else c
    pos = tt % seg
    start = (pos == seg - 1) if reverse else (pos == 0)

    @pl.when(start)
    def _():
        s_scr[...] = s0_ref[0]

    def chunks(ref, scr):
        if not colmajor:
            return ref[0]
        x = ref[0].reshape(tl, scr.shape[0] * 128)
        for t in range(scr.shape[0]):
            scr[t] = x[:, t * 128:(t + 1) * 128]
        return jnp.concatenate(
            [jnp.concatenate([scr[t, pl.ds(j, CH, stride=nc), :] for t in range(scr.shape[0])], axis=1)
             for j in range(nc)], axis=0)

    cols = chunks(cols_ref, cin_scr)
    q = cols[:, 0:GLA_QW] * (GLA_DK ** -0.5)
    k = cols[:, GLA_QW:2 * GLA_QW]
    v = cols[:, 2 * GLA_QW:2 * GLA_QW + GLA_VW]
    gd = cols[:, 2 * GLA_QW + 2 * GLA_VW:]
    z = _mm(gd, wg2_ref[...]) + bg_ref[...]
    la = -_softplus(-z) * (1.0 / GLA_TAU)
    b = _mm_exact_lhs01(tri_ref[...], la)
    tots = [b[j * CH:j * CH + 1, :] if reverse else b[(j + 1) * CH - 1:(j + 1) * CH, :] for j in range(nc)]
    totb = jnp.concatenate([jnp.broadcast_to(t, (CH, GLA_QW)) for t in tots], axis=0)
    qt = q * jnp.exp(b)
    kh = k * jnp.exp(totb - b)

    ti = lax.broadcasted_iota(jnp.int32, (CH, 128), 0)
    si = lax.broadcasted_iota(jnp.int32, (CH, 128), 1) & (CH - 1)
    incl = (si >= ti) if reverse else (si <= ti)
    lane = lax.broadcasted_iota(jnp.int32, (GLA_VP, 128), 1)
    zv = jnp.zeros((CH, GLA_VP), F32)
    units = [(j, p) for j in range(nc) for p in range(GLA_NP)]

    def subk(x, u):
        return x[u[0] * CH:(u[0] + 1) * CH, u[1] * 128:(u[1] + 1) * 128]

    def subv(x, u):
        return x[u[0] * CH:(u[0] + 1) * CH, u[1] * 256:(u[1] + 1) * 256]

    def vbd(x):
        return jnp.concatenate([jnp.concatenate([x[:, 0:GLA_VP], zv], axis=1),
                                jnp.concatenate([zv, x[:, GLA_VP:2 * GLA_VP]], axis=1)], axis=0)

    safe = jnp.min(b) >= -GLA_SAFE

    @pl.when(safe)
    def _():
        kt = k * jnp.exp(-b)
        att = {u: jnp.where(incl, _mm(subk(qt, u), subk(kt, u), _NT, bd=True), 0.0) for u in units}
        for u in units:
            oin_scr[u[0] * CH:(u[0] + 1) * CH, u[1] * 256:(u[1] + 1) * 256] = _mm(att[u], vbd(subv(v, u)))

    @pl.when(jnp.logical_not(safe))
    def _():
        q_scr[...] = q
        k_scr[...] = k
        b_scr[...] = b
        v_scr[...] = v
        sidx = lax.broadcasted_iota(jnp.int32, (CH, 1), 0)

        def row(i, _):
            t = i % CH
            base = pl.multiple_of(i - t, CH)
            msk = (sidx >= t) if reverse else (sidx <= t)
            d = jnp.where(msk, b_scr[pl.ds(i, 1), :] - b_scr[pl.ds(base, CH), :], 0.0)
            wgt = jnp.where(msk, jnp.exp(d), 0.0) * k_scr[pl.ds(base, CH), :] * q_scr[pl.ds(i, 1), :]
            vj = v_scr[pl.ds(base, CH), :]
            parts = []
            for h in range(GLA_H):
                col = jnp.sum(wgt[:, h * GLA_KP:(h + 1) * GLA_KP], axis=1, keepdims=True)
                parts.append(jnp.sum(col * vj[:, h * GLA_VP:(h + 1) * GLA_VP], axis=0, keepdims=True))
            oin_scr[pl.ds(i, 1), :] = jnp.concatenate(parts, axis=1)
            return 0

        lax.fori_loop(0, tl, row, 0)

    psi = {}
    for u in units:
        rr = _mm(subv(v, u), subk(kh, u), _TN)
        psi[u] = jnp.where(lane < 64, rr[0:GLA_VP], rr[GLA_VP:2 * GLA_VP])
    st = [s_scr[p] for p in range(GLA_NP)]
    rows = [None] * nc
    for j in (range(nc - 1, -1, -1) if reverse else range(nc)):
        o_j = []
        for p in range(GLA_NP):
            u = (j, p)
            o_j.append(_mm(subk(qt, u), st[p], _NT, bd=True))
            st[p] = st[p] * jnp.exp(tots[j][:, p * 128:(p + 1) * 128]) + psi[u]
        rows[j] = jnp.concatenate(o_j, axis=1)
    for p in range(GLA_NP):
        s_scr[p] = st[p]
        sfin_ref[0, p] = st[p]
    o = jnp.concatenate(rows, axis=0) + oin_scr[...]

    if epilogue:
        o = chunks(of_ref, io_scr) + o
        g = cols[:, 2 * GLA_QW + GLA_VW:2 * GLA_QW + 2 * GLA_VW]
        parts = []
        for h in range(GLA_H):
            oh = o[:, h * GLA_VP:(h + 1) * GLA_VP]
            ms = jnp.sum(oh * oh, axis=-1, keepdims=True) * (1.0 / GLA_DV)
            parts.append(oh * lax.rsqrt(ms + NORM_EPS))
        o = jnp.concatenate(parts, axis=1) * nw_ref[...] * _silu(g)
    if colmajor:
        for j in range(nc):
            for t in range(io_scr.shape[0]):
                io_scr[t, pl.ds(j, CH, stride=nc), :] = o[j * CH:(j + 1) * CH, t * 128:(t + 1) * 128]
        o_ref[0] = jnp.concatenate([io_scr[t] for t in range(io_scr.shape[0])], axis=1).reshape(CH, nc, GLA_VW)
    else:
        o_ref[0] = o


def _gla_dir(reverse, colmajor, cols, prm, s0, of, prev_out, seg, slab0, nb):
    ns, sl, _ = cols.shape
    nc = GLA_NC_COL if colmajor else GLA_NC
    tl = nc * CH
    nt = sl // tl
    d = 1 if reverse else 0
    epilogue = of is not None
    tidx = (lambda c: nt - 1 - c) if reverse else (lambda c: c)
    nreq = nb * (nt // seg)
    if colmajor:
        assert sl == GRID_W * CH
        view = lambda a: a.reshape(ns, CH, GRID_W // nc, nc, a.shape[-1])
        tok = lambda w: pl.BlockSpec((1, CH, None, nc, w), lambda s, c: (s + slab0, 0, tidx(c), 0, 0))
    else:
        view = lambda a: a
        tok = lambda w: pl.BlockSpec((1, tl, w), lambda s, c: (s + slab0, tidx(c), 0))
    full = lambda a: pl.BlockSpec(a.shape, lambda s, c: (0,) * a.ndim)
    sspec = pl.BlockSpec((1, GLA_NP, GLA_VP, 128), lambda s, c: (s * (nt // seg) + tidx(c) // seg, 0, 0, 0))
    ins = [view(cols), prm['wg2'][d], prm['bg'][d], prm['tri_col' if colmajor else 'tri'][d], s0]
    in_specs = [tok(GLA_PCOLS), full(ins[1]), full(ins[2]), full(ins[3]), sspec]
    if epilogue:
        ins += [view(of), prm['nw']]
        in_specs += [tok(GLA_VW), full(prm['nw'])]
    aliases = {}
    if prev_out is None:
        ins.append(jnp.zeros((8, 128), F32))
        in_specs.append(pl.BlockSpec((8, 128), lambda s, c: (0, 0)))
    else:
        ins.append(view(prev_out))
        in_specs.append(pl.BlockSpec(memory_space=pl.ANY))
        aliases = {len(ins) - 1: 0}
    o, sfin = pl.pallas_call(
        functools.partial(_gla_kernel, reverse, epilogue, colmajor, nc, seg),
        grid=(nb, nt), in_specs=in_specs,
        out_specs=[tok(GLA_VW), sspec],
        out_shape=[jax.ShapeDtypeStruct((ns, CH, GRID_W // nc, nc, GLA_VW) if colmajor else (ns, sl, GLA_VW), F32),
                   jax.ShapeDtypeStruct((nreq, GLA_NP, GLA_VP, 128), F32)],
        scratch_shapes=[pltpu.VMEM((GLA_NP, GLA_VP, 128), F32), pltpu.VMEM((tl, GLA_VW), F32),
                        pltpu.VMEM((tl, GLA_QW), F32), pltpu.VMEM((tl, GLA_QW), F32),
                        pltpu.VMEM((tl, GLA_QW), F32), pltpu.VMEM((tl, GLA_VW), F32),
                        pltpu.VMEM((GLA_PCOLS // 128, tl, 128), F32), pltpu.VMEM((GLA_VW // 128, tl, 128), F32)],
        input_output_aliases=aliases,
        compiler_params=_cp(("arbitrary", "arbitrary")),
    )(*ins)
    return o.reshape(ns, sl, GLA_VW), sfin


def _post_kernel(x_ref, ol_ref, or_ref, og_ref, m_ref, g_ref, w1_ref, w2_ref, w3_ref, wrt_ref,
                 x1_ref, h2_ref, pt_ref):
    att = (_dg(ol_ref[0].astype(BF16), w1_ref[...], _NN) + _dg(or_ref[0].astype(BF16), w2_ref[...], _NN)
           + _dg(og_ref[0].astype(BF16), w3_ref[...], _NN))
    x1 = x_ref[0] + m_ref[0, 2:3, :] * att
    x1_ref[0] = x1
    h2 = _rmsnorm_mod(x1, g_ref[...], m_ref[0, 4:5, :], m_ref[0, 3:4, :])
    h2_ref[0] = h2
    logits = lax.dot_general(wrt_ref[...], h2, _NT, precision=lax.Precision.HIGHEST,
                             preferred_element_type=F32)
    mx = jnp.max(logits, axis=0, keepdims=True)
    e = jnp.exp(logits - mx)
    pt_ref[0] = e / jnp.sum(e, axis=0, keepdims=True)


def _post(x, o_lru, o_rw, o_gla, mods, g, w1, w2, w3, wrt, tl=512):
    ns, sl, _ = x.shape
    tok = lambda c: pl.BlockSpec((1, tl, c), lambda s, t: (s, t, 0))
    full = lambda a: pl.BlockSpec(a.shape, lambda s, t: (0,) * a.ndim)
    return pl.pallas_call(
        _post_kernel,
        grid=(ns, sl // tl),
        in_specs=[tok(D), tok(D_LRU), tok(D_RW), tok(GLA_VW), pl.BlockSpec((1, 6, D), lambda s, t: (s, 0, 0)),
                  full(g), full(w1), full(w2), full(w3), full(wrt)],
        out_specs=[tok(D), tok(D), pl.BlockSpec((1, N_EXPERTS, tl), lambda s, t: (s, 0, t))],
        out_shape=[jax.ShapeDtypeStruct((ns, sl, D), F32), jax.ShapeDtypeStruct((ns, sl, D), F32),
                   jax.ShapeDtypeStruct((ns, N_EXPERTS, sl), F32)],
        compiler_params=_cp(("arbitrary", "arbitrary")),
    )(x, o_lru, o_rw, o_gla, mods, g, w1, w2, w3, wrt)


def _gather_kernel(cap, idx_ref, h_ref, o_ref, buf):
    s = pl.program_id(0)
    e = pl.program_id(1)
    base = (s * N_EXPERTS + e) * cap

    def body(j, _):
        row = idx_ref[base + j]
        buf[pl.ds(j, 1), :] = h_ref[0, pl.ds(row, 1), :]
        return 0

    lax.fori_loop(0, cap, body, 0, unroll=8)
    o_ref[0, 0] = buf[...].astype(BF16)


def _moe_gather(h2, idx_flat, cap):
    ns, sl, _ = h2.shape
    return pl.pallas_call(
        functools.partial(_gather_kernel, cap),
        grid_spec=pltpu.PrefetchScalarGridSpec(
            num_scalar_prefetch=1, grid=(ns, N_EXPERTS),
            in_specs=[pl.BlockSpec((1, sl, D), lambda s, e, idx: (s, 0, 0))],
            out_specs=pl.BlockSpec((1, 1, cap, D), lambda s, e, idx: (e, s, 0, 0)),
            scratch_shapes=[pltpu.VMEM((cap, D), F32)]),
        out_shape=jax.ShapeDtypeStruct((N_EXPERTS, ns, cap, D), BF16),
        compiler_params=_cp(("arbitrary", "arbitrary")),
    )(idx_flat, h2)


def _expert_kernel(x_ref, wg_ref, wu_ref, wd_ref, o_ref, wgb, wub, wdb):
    @pl.when(pl.program_id(1) == 0)
    def _():
        wgb[...] = wg_ref[0].astype(BF16)
        wub[...] = wu_ref[0].astype(BF16)
        wdb[...] = wd_ref[0].astype(BF16)

    x = x_ref[0, 0]
    hid = _silu(_dg(x, wgb[...], _NN)) * _dg(x, wub[...], _NN)
    o_ref[0, 0] = _dg(hid.astype(BF16), wdb[...], _NN)


def _moe_experts(xg, wg, wu, wd, l):
    ne, ns, cap, _ = xg.shape
    de = wg.shape[-1]
    wspec = lambda a: pl.BlockSpec((None, 1) + a.shape[2:], lambda e, t: (l, e, 0, 0))
    return pl.pallas_call(
        _expert_kernel,
        grid=(ne, ns),
        in_specs=[pl.BlockSpec((1, 1, cap, D), lambda e, t: (e, t, 0, 0)), wspec(wg), wspec(wu), wspec(wd)],
        out_specs=pl.BlockSpec((1, 1, cap, D), lambda e, t: (e, t, 0, 0)),
        out_shape=jax.ShapeDtypeStruct((ne, ns, cap, D), F32),
        scratch_shapes=[pltpu.VMEM((D, de), BF16), pltpu.VMEM((D, de), BF16), pltpu.VMEM((de, D), BF16)],
        compiler_params=_cp(("arbitrary", "arbitrary")),
    )(xg, wg, wu, wd)


def _scatter_kernel(cap, idx_ref, gate_ref, y_ref, o_ref):
    s = pl.program_id(0)
    e = pl.program_id(1)
    base = (s * N_EXPERTS + e) * cap

    @pl.when(e == 0)
    def _():
        o_ref[...] = jnp.zeros(o_ref.shape, F32)

    def body(j, _):
        row = idx_ref[base + j]
        gt = gate_ref[base + j]
        o_ref[0, pl.ds(row, 1), :] = o_ref[0, pl.ds(row, 1), :] + gt * y_ref[0, 0, pl.ds(j, 1), :]
        return 0

    lax.fori_loop(0, cap, body, 0, unroll=8)


def _moe_scatter(yo, idx_flat, gate_flat, sl):
    ne, ns, cap, _ = yo.shape
    return pl.pallas_call(
        functools.partial(_scatter_kernel, cap),
        grid_spec=pltpu.PrefetchScalarGridSpec(
            num_scalar_prefetch=2, grid=(ns, ne),
            in_specs=[pl.BlockSpec((1, 1, cap, D), lambda s, e, i, g: (e, s, 0, 0))],
            out_specs=pl.BlockSpec((1, sl, D), lambda s, e, i, g: (s, 0, 0))),
        out_shape=jax.ShapeDtypeStruct((ns, sl, D), F32),
        compiler_params=_cp(("arbitrary", "arbitrary")),
    )(idx_flat, gate_flat, yo)


def _cumsum_lanes(m, tri):
    wdt = tri.shape[0]
    carry = jnp.zeros((m.shape[0], 1), F32)
    outs = []
    for b0 in range(0, m.shape[1], wdt):
        c = _dg(m[:, b0:b0 + wdt].astype(BF16), tri, _NN) + carry
        outs.append(c)
        carry = c[:, wdt - 1:wdt]
    return outs[0] if len(outs) == 1 else jnp.concatenate(outs, axis=1)


def _route_kernel(cap, p_ref, tri_ref, idx_ref, gate_ref, cs_ref):
    x = p_ref[0]
    ne, n = x.shape
    bits = lax.bitcast_convert_type(x, jnp.int32)
    cur = jnp.zeros((ne, 1), jnp.int32)
    for b in range(30, -1, -1):
        t = cur | (1 << b)
        cnt = jnp.sum(jnp.where(bits >= t, 1.0, 0.0), axis=1, keepdims=True)
        cur = jnp.where(cnt >= cap, t, cur)
    gt = bits > cur
    need = cap - jnp.sum(jnp.where(gt, 1.0, 0.0), axis=1, keepdims=True)
    tri = tri_ref[...]
    eqf = jnp.where(bits == cur, 1.0, 0.0)
    rank_eq = _cumsum_lanes(eqf, tri)
    sel = jnp.where(gt, 1.0, jnp.where(rank_eq <= need, eqf, 0.0))
    cs_ref[...] = sel * _cumsum_lanes(sel, tri)

    tpos = lax.broadcasted_iota(jnp.int32, (1, n), 1)
    thi = (tpos >> 6).astype(F32)
    tlo = (tpos & 63).astype(F32)
    jcol = (lax.broadcasted_iota(jnp.int32, (cap, 1), 0) + 1).astype(F32)
    piece = min(n, 1024)

    def body(e, _):
        csr = cs_ref[pl.ds(e, 1), :]
        g = p_ref[0, pl.ds(e, 1), :]
        gh = g.astype(BF16).astype(F32)
        r1 = g - gh
        gm = r1.astype(BF16).astype(F32)
        gl = r1 - gm
        lmat = jnp.concatenate([thi, tlo, gh, gm, gl, jnp.zeros((3, n), F32)], axis=0).astype(BF16)
        acc = jnp.zeros((8, cap), F32)
        for c0 in range(0, n, piece):
            onehot = jnp.where(csr[:, c0:c0 + piece] == jcol, 1.0, 0.0).astype(BF16)
            acc = acc + _dg(lmat[:, c0:c0 + piece], onehot, _NT)
        idx_ref[0, pl.ds(e, 1), :] = (acc[0:1] * 64.0 + acc[1:2]).astype(jnp.int32)
        gate_ref[0, pl.ds(e, 1), :] = acc[2:3] + (acc[3:4] + acc[4:5])
        return 0

    lax.fori_loop(0, ne, body, 0)


def _route_call(probs, cap):
    nreq, ne, n = probs.shape
    assert n <= 4096
    wdt = min(n, 256)
    tri = jnp.asarray(np.triu(np.ones((wdt, wdt), np.float32)), BF16)
    return pl.pallas_call(
        functools.partial(_route_kernel, cap),
        grid=(nreq,),
        in_specs=[pl.BlockSpec((1, ne, n), lambda q: (q, 0, 0)), pl.BlockSpec((wdt, wdt), lambda q: (0, 0))],
        out_specs=[pl.BlockSpec((1, ne, cap), lambda q: (q, 0, 0)), pl.BlockSpec((1, ne, cap), lambda q: (q, 0, 0))],
        out_shape=[jax.ShapeDtypeStruct((nreq, ne, cap), jnp.int32), jax.ShapeDtypeStruct((nreq, ne, cap), F32)],
        scratch_shapes=[pltpu.VMEM((ne, n), F32)],
        compiler_params=_cp(("arbitrary",)),
    )(probs, tri)


def _route(probs_t, p_b, p_l):
    ns, ne, sl = probs_t.shape
    cap_p = 2 * p_l // ne
    pp = jnp.swapaxes(probs_t[0].reshape(ne, p_b, p_l), 0, 1)
    ip, gp = _route_call(pp, cap_p)
    ip = ip + (jnp.arange(p_b, dtype=jnp.int32) * p_l)[:, None, None]
    ip = jnp.swapaxes(ip, 0, 1).reshape(1, ne, p_b * cap_p)
    gp = jnp.swapaxes(gp, 0, 1).reshape(1, ne, p_b * cap_p)
    cap = 2 * sl // ne
    is_, gs = _route_call(probs_t[1:], cap)
    idx = jnp.concatenate([ip, is_], axis=0)
    gates = jnp.concatenate([gp, gs], axis=0)
    return idx.reshape(-1), gates.reshape(-1), cap


def _final_kernel(x_ref, moe_ref, m_ref, g_ref, o_ref):
    x = x_ref[0] + m_ref[0, 5:6, :] * moe_ref[0]
    o_ref[0] = (x * lax.rsqrt(jnp.mean(x * x, axis=-1, keepdims=True) + NORM_EPS)) * g_ref[...]


def _final(x, moe, mods, g, slab0, nb, tl=512):
    ns, sl, _ = x.shape
    tok = pl.BlockSpec((1, tl, D), lambda s, t: (s + slab0, t, 0))
    return pl.pallas_call(
        _final_kernel,
        grid=(nb, sl // tl),
        in_specs=[tok, tok, pl.BlockSpec((1, 6, D), lambda s, t: (s + slab0, 0, 0)),
                  pl.BlockSpec(g.shape, lambda s, t: (0, 0))],
        out_specs=pl.BlockSpec((1, tl, D), lambda s, t: (s, t, 0)),
        out_shape=jax.ShapeDtypeStruct((nb, sl, D), F32),
        compiler_params=_cp(("arbitrary", "arbitrary")),
    )(x, moe, mods, g)


def _gla_col_src():
    src = np.full((GLA_PCOLS,), -1, np.int64)
    for h in range(GLA_H):
        for j in range(GLA_DK):
            src[h * GLA_KP + j] = h * GLA_DK + j
            src[GLA_QW + h * GLA_KP + j] = GLA_H * GLA_DK + h * GLA_DK + j
        for j in range(GLA_DV):
            src[2 * GLA_QW + h * GLA_VP + j] = 2 * GLA_H * GLA_DK + h * GLA_DV + j
            src[2 * GLA_QW + GLA_VW + h * GLA_VP + j] = 2 * GLA_H * GLA_DK + D_GLA + h * GLA_DV + j
    for j in range(2 * GLA_R):
        src[2 * GLA_QW + 2 * GLA_VW + j] = 2 * GLA_H * GLA_DK + 2 * D_GLA + j
    return src


_GLA_SRC = _gla_col_src()


def _pad_cols(a, src):
    valid = jnp.asarray(src >= 0)
    return jnp.where(valid, jnp.take(a, jnp.asarray(np.maximum(src, 0)), axis=-1), 0.0)


def _key_src():
    src = np.full((GLA_QW,), -1, np.int64)
    for h in range(GLA_H):
        src[h * GLA_KP:h * GLA_KP + GLA_DK] = np.arange(h * GLA_DK, (h + 1) * GLA_DK)
    return src


def _val_src():
    src = np.full((GLA_VW,), -1, np.int64)
    for h in range(GLA_H):
        src[h * GLA_VP:h * GLA_VP + GLA_DV] = np.arange(h * GLA_DV, (h + 1) * GLA_DV)
    return src


_KEY_SRC = _key_src()
_VAL_SRC = _val_src()


def _tri_consts():
    t = np.arange(CH)
    lower = (t[None, :] <= t[:, None]).astype(np.float32)
    return jnp.asarray(lower, BF16), jnp.asarray(lower.T, BF16)


def _layer_params(l, a):
    p = {}
    w_in = a['w_in'][l]
    p['w_lru'] = w_in[:, :2 * D_LRU].astype(BF16)
    p['w_rw'] = w_in[:, 2 * D_LRU:2 * D_LRU + RW_COLS].astype(BF16)
    p['w_gla'] = _pad_cols(w_in[:, 2 * D_LRU + RW_COLS:], _GLA_SRC).astype(BF16)
    blk = jnp.asarray(np.kron(np.eye(LRU_BLOCKS, dtype=np.float32), np.ones((LRU_BW, LRU_BW), np.float32)))

    def dense(wb):
        rows = jnp.concatenate([jnp.tile(wb[n], (1, LRU_BLOCKS)) for n in range(LRU_BLOCKS)], axis=0)
        return rows * blk

    p['lru_wblk'] = jnp.concatenate([dense(a['lru_wa'][l, 0]), dense(a['lru_wx'][l, 0]),
                                     dense(a['lru_wa'][l, 1]), dense(a['lru_wx'][l, 1])], axis=1).astype(BF16)
    p['lru_bias'] = jnp.concatenate([a['lru_ba'][l, 0], a['lru_bx'][l, 0], a['lru_ba'][l, 1], a['lru_bx'][l, 1]])[None, :]
    p['lru_cw'] = a['lru_conv_w'][l]
    p['lru_cb'] = a['lru_conv_b'][l][None, :]
    p['lru_lam'] = a['lru_lambda'][l]
    lower, upper = _tri_consts()
    hones = jnp.asarray(np.kron(np.eye(RW_H, dtype=np.float32), np.ones((RW_HD, RW_HD), np.float32)), BF16)

    def dirpad(w, d):
        z = jnp.zeros_like(w)
        return jnp.concatenate([w, z] if d == 0 else [z, w], axis=0)

    p['rw'] = {
        'mu': a['rw_mu'][l],
        'w0': [a['rw_w0'][l, d][None, :] for d in range(2)],
        'w2': [dirpad(a['rw_w2'][l, d], d) for d in range(2)],
        'a0': [a['rw_a0'][l, d][None, :] for d in range(2)],
        'a2': [dirpad(a['rw_a2'][l, d], d) for d in range(2)],
        'g2': a['rw_g2'][l],
        'kkw': a['rw_kk'][l][None, :],
        'ka': a['rw_ka'][l][None, :],
        'rk': a['rw_rk'][l].reshape(2, D_RW),
        'lnw': a['rw_ln_w'][l][None, :],
        'lnb': a['rw_ln_b'][l][None, :],
        'hones': hones,
        'tri': [jnp.kron(jnp.eye(RW_NC, dtype=BF16), lower), jnp.kron(jnp.eye(RW_NC, dtype=BF16), upper)],
    }

    def gpad(w, d):
        wp = _pad_cols(w, _KEY_SRC)
        out = jnp.zeros((128, GLA_QW), F32)
        return lax.dynamic_update_slice(out, wp, (d * GLA_R, 0))

    p['gla'] = {
        'wg2': [gpad(a['gla_wg2'][l, d], d) for d in range(2)],
        'bg': [_pad_cols(a['gla_bg'][l, d][None, :], _KEY_SRC) for d in range(2)],
        'nw': _pad_cols(a['gla_norm_w'][l].reshape(1, D_GLA), _VAL_SRC),
        'tri': [jnp.kron(jnp.eye(GLA_NC, dtype=BF16), lower), jnp.kron(jnp.eye(GLA_NC, dtype=BF16), upper)],
        'tri_col': [jnp.kron(jnp.eye(GLA_NC_COL, dtype=BF16), lower), jnp.kron(jnp.eye(GLA_NC_COL, dtype=BF16), upper)],
    }
    w_out = a['w_out'][l]
    p['wo1'] = w_out[:D_LRU].astype(BF16)
    p['wo2'] = w_out[D_LRU:D_LRU + D_RW].astype(BF16)
    p['wo3'] = _pad_cols(w_out[D_LRU + D_RW:].T, _VAL_SRC).T.astype(BF16)
    p['wrt'] = a['w_router'][l].T
    p['norm1'] = a['norm1_g'][l][None, :]
    p['norm2'] = a['norm2_g'][l][None, :]
    return p


def _gla_state_in(sg):
    st = jnp.swapaxes(sg, -1, -2)
    st = jnp.pad(st, ((0, 0), (0, 0), (0, GLA_VP - GLA_DV), (0, GLA_KP - GLA_DK)))
    n = st.shape[0]
    return st.reshape(n, GLA_NP, 2, GLA_VP, GLA_KP).transpose(0, 1, 3, 2, 4).reshape(n, GLA_NP, GLA_VP, 2 * GLA_KP)


def _gla_state_out(sp):
    n = sp.shape[0]
    st = sp.reshape(n, GLA_NP, GLA_VP, 2, GLA_KP).transpose(0, 1, 3, 2, 4).reshape(n, GLA_H, GLA_VP, GLA_KP)
    return jnp.swapaxes(st[:, :, :GLA_DV, :GLA_DK], -1, -2)


def _forward(a):
    x_prompt, x_sample = a['x_prompt'], a['x_sample']
    p_b, p_l, _ = x_prompt.shape
    s_b, sl, _ = x_sample.shape
    assert p_b * p_l == sl
    ns = s_b + 1
    nreq = p_b + s_b

    cmat = jnp.zeros((16, D), F32).at[:s_b].set(a['c']).at[s_b].set(a['c_ctx'])
    mods_all = _mods(cmat, a['w_ada'], a['b_ada'])
    order = np.array([s_b] + list(range(s_b)))
    mods = mods_all[:, order].reshape(DEPTH, ns, 6, D)

    x = jnp.concatenate([x_prompt.reshape(1, sl, D), x_sample], axis=0)
    moe = None
    st_lru, st_rw, st_gla = [], [], []
    for l in range(DEPTH):
        p = _layer_params(l, a)
        x, lru_in, rw_cols, gla_cols = _pre(x, moe, mods[l - 1] if l else None, mods[l], p['norm1'],
                                            p['w_lru'], p['w_rw'], p['w_gla'])
        h0 = jnp.concatenate([jnp.zeros((1, 2, D_LRU), F32), a['state_lru'][:, l]], axis=0)
        o_lru, hfin = _lru(lru_in, p['lru_cw'], p['lru_cb'], p['lru_wblk'], p['lru_bias'], p['lru_lam'], h0, p_l, p_b)
        st_lru.append(jnp.swapaxes(hfin[0], 0, 1))
        zs = jnp.zeros((1, RW_H, RW_HD, RW_HD), F32)
        s0f = _rw_pack(jnp.concatenate([zs, a['state_rwkv'][:, l, 0]], axis=0))
        s0b = _rw_pack(jnp.concatenate([zs, a['state_rwkv'][:, l, 1]], axis=0))
        of, sf = _rwkv_dir(False, rw_cols, p['rw'], s0f, None, p_l, nreq)
        o_rw, sb = _rwkv_dir(True, rw_cols, p['rw'], s0b, of, p_l, nreq)
        st_rw.append(jnp.stack([_rw_unpack(sf[:p_b]), _rw_unpack(sb[:p_b])], axis=1))
        zg = jnp.zeros((p_b, GLA_NP, GLA_VP, 2 * GLA_KP), F32)
        gp_f, gsf = _gla_dir(False, False, gla_cols, p['gla'], zg, None, None, p_l // (GLA_NC * CH), 0, 1)
        gs_f, _ = _gla_dir(False, True, gla_cols, p['gla'], _gla_state_in(a['state_gla'][:, l, 0]), None, gp_f,
                           sl // (GLA_NC_COL * CH), 1, s_b)
        gp, gsb = _gla_dir(True, False, gla_cols, p['gla'], zg, gs_f, None, p_l // (GLA_NC * CH), 0, 1)
        o_gla, _ = _gla_dir(True, True, gla_cols, p['gla'], _gla_state_in(a['state_gla'][:, l, 1]), gs_f, gp,
                            sl // (GLA_NC_COL * CH), 1, s_b)
        st_gla.append(jnp.stack([_gla_state_out(gsf), _gla_state_out(gsb)], axis=1))
        x, h2, probs_t = _post(x, o_lru, o_rw, o_gla, mods[l], p['norm2'], p['wo1'], p['wo2'], p['wo3'], p['wrt'])
        idx, gates, cap = _route(probs_t, p_b, p_l)
        xg = _moe_gather(h2, idx, cap)
        yo = _moe_experts(xg, a['w_gate'], a['w_up'], a['w_down'], l)
        moe = _moe_scatter(yo, idx, gates, sl)
    fg = a['final_norm_g'][None, :]
    y_prompt = _final(x, moe, mods[DEPTH - 1], fg, 0, 1).reshape(p_b, p_l, D)
    y_sample = _final(x, moe, mods[DEPTH - 1], fg, 1, s_b)
    return (y_prompt, y_sample, jnp.stack(st_lru, axis=1), jnp.stack(st_rw, axis=1), jnp.stack(st_gla, axis=1))


def kernel(x_prompt, x_sample, c, state_lru, state_rwkv, state_gla, c_ctx, norm1_g, norm2_g, w_ada, b_ada, w_in, lru_conv_w, lru_conv_b, lru_wa, lru_ba, lru_wx, lru_bx, lru_lambda, rw_mu, rw_w0, rw_w2, rw_a0, rw_a2, rw_g2, rw_kk, rw_ka, rw_rk, rw_ln_w, rw_ln_b, gla_wg2, gla_bg, gla_norm_w, w_out, w_router, w_gate, w_up, w_down, final_norm_g):
    return _forward(dict(locals()))
```

```python
import functools

import numpy as np
import jax
import jax.numpy as jnp
from jax import lax
from jax.experimental import pallas as pl
from jax.experimental.pallas import tpu as pltpu

F32 = jnp.float32
BF16 = jnp.bfloat16

D = 1024
DEPTH = 2
NORM_EPS = 1e-6
D_LRU = 256
LRU_BLOCKS = 4
LRU_BW = 64
LRU_C = 8.0
D_RW = 384
RW_H = 6
RW_HD = 64
RW_COLS = 1536
RW_GN_EPS = 64e-5
D_GLA = 384
GLA_H = 4
GLA_DV = 96
GLA_DK = 48
GLA_R = 16
GLA_TAU = 16.0
N_EXPERTS = 16
GRID_W = 64

GLA_KP = 64
GLA_VP = 128
GLA_QW = GLA_H * GLA_KP
GLA_VW = GLA_H * GLA_VP
GLA_PCOLS = 2 * GLA_QW + 2 * GLA_VW + 128

CH = 64
VMEM_LIMIT = 56 * 1024 * 1024


def _cp(sem, vmem=VMEM_LIMIT):
    return pltpu.CompilerParams(dimension_semantics=sem, vmem_limit_bytes=vmem)


def _sigmoid(x):
    return 1.0 / (1.0 + jnp.exp(-x))


def _softplus(x):
    return jnp.maximum(x, 0.0) + jnp.log1p(jnp.exp(-jnp.abs(x)))


def _one_minus_exp(y):
    e = jnp.exp(y)
    em1 = e - 1.0
    small = jnp.where(em1 == 0.0, y, em1 * y / jnp.log(e))
    return -jnp.where(y < -0.5, em1, small)


def _silu(x):
    return x * _sigmoid(x)


def _gelu_tanh(x):
    return 0.5 * x * (1.0 + jnp.tanh(0.7978845608028654 * (x + 0.044715 * (x * x * x))))


_NN = (((1,), (0,)), ((), ()))
_NT = (((1,), (1,)), ((), ()))
_TN = (((0,), (0,)), ((), ()))


def _dg(a, b, dims):
    return lax.dot_general(a, b, dims, preferred_element_type=F32)


def _split2(x):
    hi = x.astype(BF16)
    lo = (x - hi.astype(F32)).astype(BF16)
    return hi, lo


def _split3(x):
    hi = x.astype(BF16)
    r = x - hi.astype(F32)
    mid = r.astype(BF16)
    lo = (r - mid.astype(F32)).astype(BF16)
    return hi, mid, lo


def _bd(x):
    lane = lax.broadcasted_iota(jnp.int32, x.shape, 1) & 127
    zero = jnp.zeros_like(x)
    return jnp.concatenate([jnp.where(lane < 64, x, zero), jnp.where(lane >= 64, x, zero)], axis=0)


def _mm(a, b, dims=_NN, passes=1, bd=False):
    expand = _bd if bd else (lambda t: t)
    if passes == 1:
        return _dg(a.astype(BF16), expand(b.astype(BF16)), dims)
    ah, al = _split2(a)
    bh, bl = _split2(b)
    bh, bl = expand(bh), expand(bl)
    ca, cb = dims[0][0][0], dims[0][1][0]
    return _dg(jnp.concatenate([ah, al, ah], axis=ca), jnp.concatenate([bh, bh, bl], axis=cb), dims)


def _mm_exact_lhs01(m01, x):
    hi, mid, lo = _split3(x)
    return _dg(m01, hi, _NN) + (_dg(m01, mid, _NN) + _dg(m01, lo, _NN))


def _head_sums(x, ones2):
    outs = []
    for p0 in range(0, x.shape[1], 128):
        hi, mid, lo = _split3(x[:, p0:p0 + 128])
        outs.append(_dg(jnp.concatenate([hi, mid, lo], axis=1), ones2, _NN))
    return jnp.concatenate(outs, axis=1)


def _rmsnorm_mod(x, g, scale, shift):
    y = x * lax.rsqrt(jnp.mean(x * x, axis=-1, keepdims=True) + NORM_EPS)
    return (y * g) * (1.0 + scale) + shift


def _mods_kernel(c_ref, w_ref, b_ref, o_ref):
    c = c_ref[...]
    o_ref[0] = _mm(_silu(c), w_ref[0]) + b_ref[0]


def _mods(cmat, w_ada, b_ada):
    nt = 4
    tn = 6 * D // nt
    return pl.pallas_call(
        _mods_kernel,
        grid=(DEPTH, nt),
        in_specs=[pl.BlockSpec((16, D), lambda l, j: (0, 0)),
                  pl.BlockSpec((1, D, tn), lambda l, j: (l, 0, j)),
                  pl.BlockSpec((1, 1, tn), lambda l, j: (l, 0, j))],
        out_specs=pl.BlockSpec((1, 16, tn), lambda l, j: (l, 0, j)),
        out_shape=jax.ShapeDtypeStruct((DEPTH, 16, 6 * D), F32),
        compiler_params=_cp(("arbitrary", "arbitrary")),
    )(cmat, w_ada, b_ada.reshape(DEPTH, 1, 6 * D))


def _pre_kernel(has_res, *refs):
    if has_res:
        (x_ref, moe_ref, pm_ref, m_ref, g_ref, wl_ref, wr_ref, wg_ref,
         xo_ref, lru_ref, rw_ref, gla_ref) = refs
        x = x_ref[0] + pm_ref[0, 5:6, :] * moe_ref[0]
        xo_ref[0] = x
    else:
        x_ref, m_ref, g_ref, wl_ref, wr_ref, wg_ref, lru_ref, rw_ref, gla_ref = refs
        x = x_ref[0]
    h = _rmsnorm_mod(x, g_ref[...], m_ref[0, 1:2, :], m_ref[0, 0:1, :]).astype(BF16)
    lru_ref[0] = _dg(h, wl_ref[...], _NN)
    rw_ref[0] = _dg(h, wr_ref[...], _NN)
    gla_ref[0] = _dg(h, wg_ref[...], _NN)


def _pre(x, moe, prev_mods, mods, g, wl, wr, wg, tl=512):
    ns, sl, _ = x.shape
    has_res = moe is not None
    tok = lambda c: pl.BlockSpec((1, tl, c), lambda s, t: (s, t, 0))
    mspec = pl.BlockSpec((1, 6, D), lambda s, t: (s, 0, 0))
    full = lambda a: pl.BlockSpec(a.shape, lambda s, t: (0,) * a.ndim)
    ins = [x] + ([moe, prev_mods] if has_res else []) + [mods, g, wl, wr, wg]
    in_specs = [tok(D)] + ([tok(D), mspec] if has_res else []) + [mspec, full(g), full(wl), full(wr), full(wg)]
    outs = [jax.ShapeDtypeStruct((ns, sl, c), F32) for c in (2 * D_LRU, RW_COLS, GLA_PCOLS)]
    out_specs = [tok(2 * D_LRU), tok(RW_COLS), tok(GLA_PCOLS)]
    if has_res:
        outs = [jax.ShapeDtypeStruct((ns, sl, D), F32)] + outs
        out_specs = [tok(D)] + out_specs
    res = pl.pallas_call(
        functools.partial(_pre_kernel, has_res),
        grid=(ns, sl // tl), in_specs=in_specs, out_specs=out_specs, out_shape=outs,
        compiler_params=_cp(("arbitrary", "arbitrary")),
    )(*ins)
    if has_res:
        return res
    return [x] + list(res)


def _lru_kernel(tl, seg0, segn, in_ref, cw_ref, cb_ref, wb_ref, bias_ref, lam_ref, h0_ref,
                o_ref, hfin_ref, xp_ref, hf_ref):
    s = pl.program_id(0)
    sl = in_ref.shape[1]
    nt = sl // tl
    seg = jnp.where(s == 0, seg0, segn)
    c = D_LRU

    xp_ref[0:8, :] = jnp.zeros((8, c), F32)
    xp_ref[sl + 8:sl + 16, :] = jnp.zeros((8, c), F32)
    xp_ref[8:sl + 8, :] = in_ref[0, :, 0:c]
    hfin_ref[...] = jnp.zeros(hfin_ref.shape, F32)

    cw = cw_ref[...]
    cb = cb_ref[...]
    sp = _softplus(-lam_ref[...])
    rid = lax.broadcasted_iota(jnp.int32, (tl, 1), 0)
    wid = lax.broadcasted_iota(jnp.int32, (tl + 16, 1), 0)

    def gates(i, d):
        t0 = pl.multiple_of(i * tl, tl)
        pos = i % seg
        win = xp_ref[pl.ds(t0, tl + 16), :]
        lo = jnp.where(pos == 0, 8, 0)
        hi = jnp.where(pos == seg - 1, tl + 8, tl + 16)
        win = jnp.where(wid >= lo, jnp.where(wid < hi, win, 0.0), 0.0)
        xc = cb + cw[0:1] * win[6:6 + tl] + cw[1:2] * win[7:7 + tl] + cw[2:3] * win[8:8 + tl] + cw[3:4] * win[9:9 + tl]
        gt = _dg(xc.astype(BF16), wb_ref[:, 2 * c * d:2 * c * (d + 1)], _NN) + bias_ref[:, 2 * c * d:2 * c * (d + 1)]
        r = _sigmoid(gt[:, 0:c])
        ig = _sigmoid(gt[:, c:2 * c])
        log_a = (-LRU_C) * r * sp[d:d + 1]
        a = jnp.exp(log_a)
        u = jnp.sqrt(_one_minus_exp(2.0 * log_a)) * (ig * xc)
        return t0, pos, a, u

    def scan(a, u, rev):
        k = 1
        while k < tl:
            if rev:
                ok = rid < tl - k
                a_s = jnp.where(ok, pltpu.roll(a, tl - k, 0), 1.0)
                u_s = jnp.where(ok, pltpu.roll(u, tl - k, 0), 0.0)
            else:
                ok = rid >= k
                a_s = jnp.where(ok, pltpu.roll(a, k, 0), 1.0)
                u_s = jnp.where(ok, pltpu.roll(u, k, 0), 0.0)
            u = a * u_s + u
            a = a * a_s
            k *= 2
        return a, u

    def fwd(i, carry):
        t0, pos, a, u = gates(i, 0)
        a, u = scan(a, u, False)
        carry = jnp.where(pos == 0, h0_ref[0, 0:1, :], carry)
        h = a * carry + u
        hf_ref[pl.ds(t0, tl), :] = h
        last = h[tl - 1:tl, :]

        @pl.when(pos == seg - 1)
        def _():
            hfin_ref[0, 0, pl.ds(i // seg, 1), :] = last
        return last

    lax.fori_loop(0, nt, fwd, jnp.zeros((1, c), F32))

    def bwd(j, carry):
        i = nt - 1 - j
        t0, pos, a, u = gates(i, 1)
        a, u = scan(a, u, True)
        carry = jnp.where(pos == seg - 1, h0_ref[0, 1:2, :], carry)
        h = a * carry + u
        y = in_ref[0, pl.ds(t0, tl), c:2 * c]
        o_ref[0, pl.ds(t0, tl), :] = (hf_ref[pl.ds(t0, tl), :] + h) * _gelu_tanh(y)
        first = h[0:1, :]

        @pl.when(pos == 0)
        def _():
            hfin_ref[0, 1, pl.ds(i // seg, 1), :] = first
        return first

    lax.fori_loop(0, nt, bwd, jnp.zeros((1, c), F32))


def _lru(lru_in, cw, cb, wblk, bias, lam, h0, p_l, nreq0, tl=256):
    ns, sl, _ = lru_in.shape
    full = lambda a: pl.BlockSpec(a.shape, lambda s: (0,) * a.ndim)
    return pl.pallas_call(
        functools.partial(_lru_kernel, tl, p_l // tl, sl // tl),
        grid=(ns,),
        in_specs=[pl.BlockSpec((1, sl, 2 * D_LRU), lambda s: (s, 0, 0)),
                  full(cw), full(cb), full(wblk), full(bias), full(lam),
                  pl.BlockSpec((1, 2, D_LRU), lambda s: (s, 0, 0))],
        out_specs=[pl.BlockSpec((1, sl, D_LRU), lambda s: (s, 0, 0)),
                   pl.BlockSpec((1, 2, nreq0, D_LRU), lambda s: (s, 0, 0, 0))],
        out_shape=[jax.ShapeDtypeStruct((ns, sl, D_LRU), F32),
                   jax.ShapeDtypeStruct((ns, 2, nreq0, D_LRU), F32)],
        scratch_shapes=[pltpu.VMEM((sl + 16, D_LRU), F32), pltpu.VMEM((sl, D_LRU), F32)],
        compiler_params=_cp(("arbitrary",)),
    )(lru_in, cw, cb, wblk, bias, lam, h0)


RW_NC = 4
RW_P_INV = 3
RW_P_ATT = 1
RW_NP = RW_H // 2


def _rwkv_kernel(reverse, epilogue, nc, seg0, segn, *refs):
    if epilogue:
        (cols_ref, prev_ref, next_ref, mu_ref, w0_ref, w2_ref, a0_ref, a2_ref, g2_ref, kkw_ref, ka_ref,
         hones_ref, tri_ref, s0_ref, of_ref, a0o_ref, a2o_ref, rk_ref, lnw_ref, lnb_ref,
         o_ref, sfin_ref, s_scr) = refs
    else:
        (cols_ref, prev_ref, next_ref, mu_ref, w0_ref, w2_ref, a0_ref, a2_ref, g2_ref, kkw_ref, ka_ref,
         hones_ref, tri_ref, s0_ref, o_ref, sfin_ref, s_scr) = refs
    tl = nc * CH
    s = pl.program_id(0)
    c = pl.program_id(1)
    nt = pl.num_programs(1)
    tt = nt - 1 - c if reverse else c
    seg = jnp.where(s == 0, seg0, segn)
    pos = tt % seg
    first = pos == 0
    last = pos == seg - 1
    start = last if reverse else first

    @pl.when(start)
    def _():
        s_scr[...] = s0_ref[0]

    cols = cols_ref[0]
    prow = jnp.where(first, 0.0, prev_ref[0, 7:8, :])
    nrow = jnp.where(last, 0.0, next_ref[0, 0:1, :])
    rid = lax.broadcasted_iota(jnp.int32, (tl, 1), 0)
    prev = jnp.where(rid == 0, prow, pltpu.roll(cols, 1, 0))
    nxt = jnp.where(rid == tl - 1, nrow, pltpu.roll(cols, tl - 1, 0))
    xs = cols + mu_ref[0:1, :] * (prev - cols) + mu_ref[1:2, :] * (nxt - cols)

    w = D_RW
    r = xs[:, 0:w]
    k = xs[:, w:2 * w]
    v = xs[:, 2 * w:3 * w]
    wd = xs[:, 3 * w:3 * w + 128]
    ad = xs[:, 3 * w + 128:3 * w + 256]
    gd = xs[:, 3 * w + 256:3 * w + 384]

    hones = hones_ref[...]
    wlog = -_softplus(-(w0_ref[...] + _mm(jnp.tanh(wd), w2_ref[...]))) - 0.5
    lw = -jnp.exp(wlog)
    alpha = _sigmoid(a0_ref[...] + _mm(ad, a2_ref[...]))
    kk = k * kkw_ref[...]
    kk = kk * lax.rsqrt(_head_sums(kk * kk, hones) + 1e-12)
    kd = k * (1.0 + (alpha - 1.0) * ka_ref[...])

    bw = _mm_exact_lhs01(tri_ref[...], lw)
    tots = [bw[j * CH:j * CH + 1, :] if reverse else bw[(j + 1) * CH - 1:(j + 1) * CH, :] for j in range(nc)]
    totb = jnp.concatenate([jnp.broadcast_to(t, (CH, w)) for t in tots], axis=0)
    kal = kk * alpha
    rt = r * jnp.exp(bw)
    at = -kk * jnp.exp(bw - lw)
    en = jnp.exp(-bw)
    bt = kal * en
    kt = kd * en
    ec = jnp.exp(totb - bw)
    bh = kal * ec
    kh = kd * ec

    ti = lax.broadcasted_iota(jnp.int32, (CH, 128), 0)
    si = lax.broadcasted_iota(jnp.int32, (CH, 128), 1) & (CH - 1)
    strict = (si > ti) if reverse else (si < ti)
    incl = (si >= ti) if reverse else (si <= ti)
    eye2 = jnp.where(si == ti, 1.0, 0.0)
    lane = lax.broadcasted_iota(jnp.int32, (CH, 128), 1)
    ri = lax.broadcasted_iota(jnp.int32, (128, 128), 0)
    ci = lax.broadcasted_iota(jnp.int32, (128, 128), 1)
    same_head = (ri < 64) == (ci < 64)
    diag = ri == ci

    units = [(j, p) for j in range(nc) for p in range(RW_NP)]

    def sub(x, u):
        j, p = u
        return x[j * CH:(j + 1) * CH, p * 128:(p + 1) * 128]

    n_, rb_, ak_, rk_ = {}, {}, {}, {}
    for u in units:
        lhs = jnp.concatenate([sub(at, u), sub(rt, u)], axis=0)
        gb = _mm(lhs, sub(bt, u), _NT, RW_P_ATT, bd=True)
        gk = _mm(lhs, sub(kt, u), _NT, RW_P_ATT, bd=True)
        n_[u] = jnp.where(strict, gb[0:CH], 0.0)
        rb_[u] = jnp.where(incl, gb[CH:2 * CH], 0.0)
        ak_[u] = jnp.where(strict, gk[0:CH], 0.0)
        rk_[u] = jnp.where(incl, gk[CH:2 * CH], 0.0)
    av_ = {u: _mm(ak_[u], sub(v, u), _NN, RW_P_ATT, bd=True) for u in units}
    tp_ = {u: eye2 + n_[u] for u in units}
    np_ = {u: _mm(n_[u], n_[u], _NN, RW_P_INV, bd=True) for u in units}
    p2 = 2
    while 2 * p2 < CH:
        both = {u: _mm(jnp.concatenate([np_[u], tp_[u]], axis=0), np_[u], _NN, RW_P_INV, bd=True) for u in units}
        tp_ = {u: tp_[u] + both[u][CH:2 * CH] for u in units}
        np_ = {u: both[u][0:CH] for u in units}
        p2 *= 2
    tp_ = {u: tp_[u] + _mm(tp_[u], np_[u], _NN, RW_P_INV, bd=True) for u in units}
    pq_ = {u: _mm(tp_[u], jnp.concatenate([sub(at, u), av_[u]], axis=1), _NN, RW_P_ATT, bd=True) for u in units}
    om_, oi_, phi_, psi_ = {}, {}, {}, {}
    for u in units:
        pp, qq = pq_[u][:, 0:128], pq_[u][:, 128:256]
        x2 = _mm(rb_[u], pq_[u], _NN, RW_P_ATT, bd=True)
        om_[u] = sub(rt, u) + x2[:, 0:128]
        oi_[u] = x2[:, 128:256] + _mm(rk_[u], sub(v, u), _NN, RW_P_ATT, bd=True)
        wc = jnp.exp(tots[u[0]][:, u[1] * 128:(u[1] + 1) * 128])
        phi_[u] = jnp.where(same_head, _mm(pp, sub(bh, u), _TN, RW_P_ATT), 0.0) + jnp.where(diag, wc, 0.0)
        rr = _mm(qq, sub(bh, u), _TN, RW_P_ATT) + _mm(sub(v, u), sub(kh, u), _TN, RW_P_ATT)
        psi_[u] = jnp.where(lane < 64, rr[0:CH], rr[CH:2 * CH])

    st = [s_scr[p] for p in range(RW_NP)]
    rows = [None] * nc
    for j in (range(nc - 1, -1, -1) if reverse else range(nc)):
        o_j = []
        for p in range(RW_NP):
            u = (j, p)
            o_j.append(_mm(om_[u], st[p], _NT, RW_P_ATT, bd=True) + oi_[u])
            st[p] = _mm(st[p], phi_[u], _NN, RW_P_ATT) + psi_[u]
        rows[j] = jnp.concatenate(o_j, axis=1)
    for p in range(RW_NP):
        s_scr[p] = st[p]
        sfin_ref[0, p] = st[p]
    o = jnp.concatenate(rows, axis=0)

    if epilogue:
        o = of_ref[0] + o
        inv = 1.0 / RW_HD
        mean = _head_sums(o, hones) * inv
        dlt = o - mean
        var = _head_sums(dlt * dlt, hones) * inv
        on = dlt * lax.rsqrt(var + RW_GN_EPS) * lnw_ref[...] + lnb_ref[...]
        alpha_o = _sigmoid(a0o_ref[...] + _mm(ad, a2o_ref[...]))
        kd_o = k * (1.0 + (alpha_o - 1.0) * ka_ref[...])
        d_here, d_oth = (1, 0) if reverse else (0, 1)
        rk = rk_ref[...]
        bonus = _head_sums(r * (kd * rk[d_here:d_here + 1] + kd_o * rk[d_oth:d_oth + 1]), hones) * v
        g = _mm(_sigmoid(gd), g2_ref[...])
        o = (on + bonus) * g
    o_ref[0] = o


def _rwkv_dir(reverse, cols, prm, s0, of, p_l, nreq):
    ns, sl, _ = cols.shape
    tl = RW_NC * CH
    nt = sl // tl
    d = 1 if reverse else 0
    epilogue = of is not None
    tidx = (lambda c: nt - 1 - c) if reverse else (lambda c: c)
    nb8 = sl // 8
    r8 = tl // 8
    full = lambda a: pl.BlockSpec(a.shape, lambda s, c: (0,) * a.ndim)
    seg0 = p_l // tl

    def req(s, c):
        return jnp.where(s == 0, tidx(c) // seg0, nreq - ns + s)

    ins = [cols, cols, cols, prm['mu'], prm['w0'][d], prm['w2'][d], prm['a0'][d], prm['a2'][d], prm['g2'],
           prm['kkw'], prm['ka'], prm['hones'], prm['tri'][d], s0]
    in_specs = [pl.BlockSpec((1, tl, RW_COLS), lambda s, c: (s, tidx(c), 0)),
                pl.BlockSpec((1, 8, RW_COLS), lambda s, c: (s, jnp.maximum(tidx(c) * r8 - 1, 0), 0)),
                pl.BlockSpec((1, 8, RW_COLS), lambda s, c: (s, jnp.minimum(tidx(c) * r8 + r8, nb8 - 1), 0))]
    in_specs += [full(a) for a in ins[3:13]]
    in_specs += [pl.BlockSpec((1, RW_NP, RW_HD, 128), lambda s, c: (s, 0, 0, 0))]
    if epilogue:
        ins += [of, prm['a0'][1 - d], prm['a2'][1 - d], prm['rk'], prm['lnw'], prm['lnb']]
        in_specs += [pl.BlockSpec((1, tl, D_RW), lambda s, c: (s, tidx(c), 0))]
        in_specs += [full(a) for a in ins[15:]]
    return pl.pallas_call(
        functools.partial(_rwkv_kernel, reverse, epilogue, RW_NC, seg0, nt),
        grid=(ns, nt), in_specs=in_specs,
        out_specs=[pl.BlockSpec((1, tl, D_RW), lambda s, c: (s, tidx(c), 0)),
                   pl.BlockSpec((1, RW_NP, RW_HD, 128), lambda s, c: (req(s, c), 0, 0, 0))],
        out_shape=[jax.ShapeDtypeStruct((ns, sl, D_RW), F32),
                   jax.ShapeDtypeStruct((nreq, RW_NP, RW_HD, 128), F32)],
        scratch_shapes=[pltpu.VMEM((RW_NP, RW_HD, 128), F32)],
        compiler_params=_cp(("arbitrary", "arbitrary")),
    )(*ins)


def _rw_pack(st):
    n = st.shape[0]
    return st.reshape(n, RW_NP, 2, RW_HD, RW_HD).transpose(0, 1, 3, 2, 4).reshape(n, RW_NP, RW_HD, 2 * RW_HD)


def _rw_unpack(sp):
    n = sp.shape[0]
    return sp.reshape(n, RW_NP, RW_HD, 2, RW_HD).transpose(0, 1, 3, 2, 4).reshape(n, RW_H, RW_HD, RW_HD)


GLA_NC = 4
GLA_NC_COL = 8
GLA_NP = GLA_H // 2
GLA_SAFE = 40.0


def _gla_kernel(reverse, epilogue, colmajor, nc, seg, *refs):
    if epilogue:
        (cols_ref, wg2_ref, bg_ref, tri_ref, s0_ref, of_ref, nw_ref, prev_ref,
         o_ref, sfin_ref, s_scr, oin_scr, q_scr, k_scr, b_scr, v_scr, cin_scr, io_scr) = refs
    else:
        (cols_ref, wg2_ref, bg_ref, tri_ref, s0_ref, prev_ref,
         o_ref, sfin_ref, s_scr, oin_scr, q_scr, k_scr, b_scr, v_scr, cin_scr, io_scr) = refs
    del prev_ref
    tl = nc * CH
    c = pl.program_id(1)
    nt = pl.num_programs(1)
    tt = nt - 1 - c if reverse else c
    pos = tt % seg
    start = (pos == seg - 1) if reverse else (pos == 0)

    @pl.when(start)
    def _():
        s_scr[...] = s0_ref[0]

    def chunks(ref, scr):
        if not colmajor:
            return ref[0]
        x = ref[0].reshape(tl, scr.shape[0] * 128)
        for t in range(scr.shape[0]):
            scr[t] = x[:, t * 128:(t + 1) * 128]
        return jnp.concatenate(
            [jnp.concatenate([scr[t, pl.ds(j, CH, stride=nc), :] for t in range(scr.shape[0])], axis=1)
             for j in range(nc)], axis=0)

    cols = chunks(cols_ref, cin_scr)
    q = cols[:, 0:GLA_QW] * (GLA_DK ** -0.5)
    k = cols[:, GLA_QW:2 * GLA_QW]
    v = cols[:, 2 * GLA_QW:2 * GLA_QW + GLA_VW]
    gd = cols[:, 2 * GLA_QW + 2 * GLA_VW:]
    z = _mm(gd, wg2_ref[...]) + bg_ref[...]
    la = -_softplus(-z) * (1.0 / GLA_TAU)
    b = _mm_exact_lhs01(tri_ref[...], la)
    tots = [b[j * CH:j * CH + 1, :] if reverse else b[(j + 1) * CH - 1:(j + 1) * CH, :] for j in range(nc)]
    totb = jnp.concatenate([jnp.broadcast_to(t, (CH, GLA_QW)) for t in tots], axis=0)
    qt = q * jnp.exp(b)
    kh = k * jnp.exp(totb - b)

    ti = lax.broadcasted_iota(jnp.int32, (CH, 128), 0)
    si = lax.broadcasted_iota(jnp.int32, (CH, 128), 1) & (CH - 1)
    incl = (si >= ti) if reverse else (si <= ti)
    lane = lax.broadcasted_iota(jnp.int32, (GLA_VP, 128), 1)
    zv = jnp.zeros((CH, GLA_VP), F32)
    units = [(j, p) for j in range(nc) for p in range(GLA_NP)]

    def subk(x, u):
        return x[u[0] * CH:(u[0] + 1) * CH, u[1] * 128:(u[1] + 1) * 128]

    def subv(x, u):
        return x[u[0] * CH:(u[0] + 1) * CH, u[1] * 256:(u[1] + 1) * 256]

    def vbd(x):
        return jnp.concatenate([jnp.concatenate([x[:, 0:GLA_VP], zv], axis=1),
                                jnp.concatenate([zv, x[:, GLA_VP:2 * GLA_VP]], axis=1)], axis=0)

    safe = jnp.min(b) >= -GLA_SAFE

    @pl.when(safe)
    def _():
        kt = k * jnp.exp(-b)
        att = {u: jnp.where(incl, _mm(subk(qt, u), subk(kt, u), _NT, bd=True), 0.0) for u in units}
        for u in units:
            oin_scr[u[0] * CH:(u[0] + 1) * CH, u[1] * 256:(u[1] + 1) * 256] = _mm(att[u], vbd(subv(v, u)))

    @pl.when(jnp.logical_not(safe))
    def _():
        q_scr[...] = q
        k_scr[...] = k
        b_scr[...] = b
        v_scr[...] = v
        sidx = lax.broadcasted_iota(jnp.int32, (CH, 1), 0)

        def row(i, _):
            t = i % CH
            base = pl.multiple_of(i - t, CH)
            msk = (sidx >= t) if reverse else (sidx <= t)
            d = jnp.where(msk, b_scr[pl.ds(i, 1), :] - b_scr[pl.ds(base, CH), :], 0.0)
            wgt = jnp.where(msk, jnp.exp(d), 0.0) * k_scr[pl.ds(base, CH), :] * q_scr[pl.ds(i, 1), :]
            vj = v_scr[pl.ds(base, CH), :]
            parts = []
            for h in range(GLA_H):
                col = jnp.sum(wgt[:, h * GLA_KP:(h + 1) * GLA_KP], axis=1, keepdims=True)
                parts.append(jnp.sum(col * vj[:, h * GLA_VP:(h + 1) * GLA_VP], axis=0, keepdims=True))
            oin_scr[pl.ds(i, 1), :] = jnp.concatenate(parts, axis=1)
            return 0

        lax.fori_loop(0, tl, row, 0)

    psi = {}
    for u in units:
        rr = _mm(subv(v, u), subk(kh, u), _TN)
        psi[u] = jnp.where(lane < 64, rr[0:GLA_VP], rr[GLA_VP:2 * GLA_VP])
    st = [s_scr[p] for p in range(GLA_NP)]
    rows = [None] * nc
    for j in (range(nc - 1, -1, -1) if reverse else range(nc)):
        o_j = []
        for p in range(GLA_NP):
            u = (j, p)
            o_j.append(_mm(subk(qt, u), st[p], _NT, bd=True))
            st[p] = st[p] * jnp.exp(tots[j][:, p * 128:(p + 1) * 128]) + psi[u]
        rows[j] = jnp.concatenate(o_j, axis=1)
    for p in range(GLA_NP):
        s_scr[p] = st[p]
        sfin_ref[0, p] = st[p]
    o = jnp.concatenate(rows, axis=0) + oin_scr[...]

    if epilogue:
        o = chunks(of_ref, io_scr) + o
        g = cols[:, 2 * GLA_QW + GLA_VW:2 * GLA_QW + 2 * GLA_VW]
        parts = []
        for h in range(GLA_H):
            oh = o[:, h * GLA_VP:(h + 1) * GLA_VP]
            ms = jnp.sum(oh * oh, axis=-1, keepdims=True) * (1.0 / GLA_DV)
            parts.append(oh * lax.rsqrt(ms + NORM_EPS))
        o = jnp.concatenate(parts, axis=1) * nw_ref[...] * _silu(g)
    if colmajor:
        for j in range(nc):
            for t in range(io_scr.shape[0]):
                io_scr[t, pl.ds(j, CH, stride=nc), :] = o[j * CH:(j + 1) * CH, t * 128:(t + 1) * 128]
        o_ref[0] = jnp.concatenate([io_scr[t] for t in range(io_scr.shape[0])], axis=1).reshape(CH, nc, GLA_VW)
    else:
        o_ref[0] = o


def _gla_dir(reverse, colmajor, cols, prm, s0, of, prev_out, seg, slab0, nb):
    ns, sl, _ = cols.shape
    nc = GLA_NC_COL if colmajor else GLA_NC
    tl = nc * CH
    nt = sl // tl
    d = 1 if reverse else 0
    epilogue = of is not None
    tidx = (lambda c: nt - 1 - c) if reverse else (lambda c: c)
    nreq = nb * (nt // seg)
    if colmajor:
        assert sl == GRID_W * CH
        view = lambda a: a.reshape(ns, CH, GRID_W // nc, nc, a.shape[-1])
        tok = lambda w: pl.BlockSpec((1, CH, None, nc, w), lambda s, c: (s + slab0, 0, tidx(c), 0, 0))
    else:
        view = lambda a: a
        tok = lambda w: pl.BlockSpec((1, tl, w), lambda s, c: (s + slab0, tidx(c), 0))
    full = lambda a: pl.BlockSpec(a.shape, lambda s, c: (0,) * a.ndim)
    sspec = pl.BlockSpec((1, GLA_NP, GLA_VP, 128), lambda s, c: (s * (nt // seg) + tidx(c) // seg, 0, 0, 0))
    ins = [view(cols), prm['wg2'][d], prm['bg'][d], prm['tri_col' if colmajor else 'tri'][d], s0]
    in_specs = [tok(GLA_PCOLS), full(ins[1]), full(ins[2]), full(ins[3]), sspec]
    if epilogue:
        ins += [view(of), prm['nw']]
        in_specs += [tok(GLA_VW), full(prm['nw'])]
    aliases = {}
    if prev_out is None:
        ins.append(jnp.zeros((8, 128), F32))
        in_specs.append(pl.BlockSpec((8, 128), lambda s, c: (0, 0)))
    else:
        ins.append(view(prev_out))
        in_specs.append(pl.BlockSpec(memory_space=pl.ANY))
        aliases = {len(ins) - 1: 0}
    o, sfin = pl.pallas_call(
        functools.partial(_gla_kernel, reverse, epilogue, colmajor, nc, seg),
        grid=(nb, nt), in_specs=in_specs,
        out_specs=[tok(GLA_VW), sspec],
        out_shape=[jax.ShapeDtypeStruct((ns, CH, GRID_W // nc, nc, GLA_VW) if colmajor else (ns, sl, GLA_VW), F32),
                   jax.ShapeDtypeStruct((nreq, GLA_NP, GLA_VP, 128), F32)],
        scratch_shapes=[pltpu.VMEM((GLA_NP, GLA_VP, 128), F32), pltpu.VMEM((tl, GLA_VW), F32),
                        pltpu.VMEM((tl, GLA_QW), F32), pltpu.VMEM((tl, GLA_QW), F32),
                        pltpu.VMEM((tl, GLA_QW), F32), pltpu.VMEM((tl, GLA_VW), F32),
                        pltpu.VMEM((GLA_PCOLS // 128, tl, 128), F32), pltpu.VMEM((GLA_VW // 128, tl, 128), F32)],
        input_output_aliases=aliases,
        compiler_params=_cp(("arbitrary", "arbitrary")),
    )(*ins)
    return o.reshape(ns, sl, GLA_VW), sfin


def _post_kernel(x_ref, ol_ref, or_ref, og_ref, m_ref, g_ref, w1_ref, w2_ref, w3_ref, wrt_ref,
                 x1_ref, h2_ref, pt_ref):
    att = (_dg(ol_ref[0].astype(BF16), w1_ref[...], _NN) + _dg(or_ref[0].astype(BF16), w2_ref[...], _NN)
           + _dg(og_ref[0].astype(BF16), w3_ref[...], _NN))
    x1 = x_ref[0] + m_ref[0, 2:3, :] * att
    x1_ref[0] = x1
    h2 = _rmsnorm_mod(x1, g_ref[...], m_ref[0, 4:5, :], m_ref[0, 3:4, :])
    h2_ref[0] = h2
    logits = lax.dot_general(wrt_ref[...], h2, _NT, precision=lax.Precision.HIGHEST,
                             preferred_element_type=F32)
    mx = jnp.max(logits, axis=0, keepdims=True)
    e = jnp.exp(logits - mx)
    pt_ref[0] = e / jnp.sum(e, axis=0, keepdims=True)


def _post(x, o_lru, o_rw, o_gla, mods, g, w1, w2, w3, wrt, tl=512):
    ns, sl, _ = x.shape
    tok = lambda c: pl.BlockSpec((1, tl, c), lambda s, t: (s, t, 0))
    full = lambda a: pl.BlockSpec(a.shape, lambda s, t: (0,) * a.ndim)
    return pl.pallas_call(
        _post_kernel,
        grid=(ns, sl // tl),
        in_specs=[tok(D), tok(D_LRU), tok(D_RW), tok(GLA_VW), pl.BlockSpec((1, 6, D), lambda s, t: (s, 0, 0)),
                  full(g), full(w1), full(w2), full(w3), full(wrt)],
        out_specs=[tok(D), tok(D), pl.BlockSpec((1, N_EXPERTS, tl), lambda s, t: (s, 0, t))],
        out_shape=[jax.ShapeDtypeStruct((ns, sl, D), F32), jax.ShapeDtypeStruct((ns, sl, D), F32),
                   jax.ShapeDtypeStruct((ns, N_EXPERTS, sl), F32)],
        compiler_params=_cp(("arbitrary", "arbitrary")),
    )(x, o_lru, o_rw, o_gla, mods, g, w1, w2, w3, wrt)


GATHER_ROWS = 16


def _gather_kernel(cap, idx_ref, h_ref, o_ref):
    s = pl.program_id(0)
    e = pl.program_id(1)
    base = (s * N_EXPERTS + e) * cap

    def body(g, _):
        j0 = pl.multiple_of(g * GATHER_ROWS, GATHER_ROWS)
        rows = [h_ref[0, pl.ds(idx_ref[base + j0 + i], 1), :] for i in range(GATHER_ROWS)]
        o_ref[0, 0, pl.ds(j0, GATHER_ROWS), :] = jnp.concatenate(rows, axis=0).astype(BF16)
        return 0

    lax.fori_loop(0, cap // GATHER_ROWS, body, 0, unroll=4)


def _moe_gather(h2, idx_flat, cap):
    ns, sl, _ = h2.shape
    return pl.pallas_call(
        functools.partial(_gather_kernel, cap),
        grid_spec=pltpu.PrefetchScalarGridSpec(
            num_scalar_prefetch=1, grid=(ns, N_EXPERTS),
            in_specs=[pl.BlockSpec((1, sl, D), lambda s, e, idx: (s, 0, 0))],
            out_specs=pl.BlockSpec((1, 1, cap, D), lambda s, e, idx: (e, s, 0, 0))),
        out_shape=jax.ShapeDtypeStruct((N_EXPERTS, ns, cap, D), BF16),
        compiler_params=_cp(("arbitrary", "arbitrary")),
    )(idx_flat, h2)


def _expert_kernel(x_ref, wg_ref, wu_ref, wd_ref, o_ref, wgb, wub, wdb):
    @pl.when(pl.program_id(1) == 0)
    def _():
        wgb[...] = wg_ref[0].astype(BF16)
        wub[...] = wu_ref[0].astype(BF16)
        wdb[...] = wd_ref[0].astype(BF16)

    x = x_ref[0, 0]
    hid = _silu(_dg(x, wgb[...], _NN)) * _dg(x, wub[...], _NN)
    o_ref[0, 0] = _dg(hid.astype(BF16), wdb[...], _NN)


def _moe_experts(xg, wg, wu, wd, l):
    ne, ns, cap, _ = xg.shape
    de = wg.shape[-1]
    wspec = lambda a: pl.BlockSpec((None, 1) + a.shape[2:], lambda e, t: (l, e, 0, 0))
    return pl.pallas_call(
        _expert_kernel,
        grid=(ne, ns),
        in_specs=[pl.BlockSpec((1, 1, cap, D), lambda e, t: (e, t, 0, 0)), wspec(wg), wspec(wu), wspec(wd)],
        out_specs=pl.BlockSpec((1, 1, cap, D), lambda e, t: (e, t, 0, 0)),
        out_shape=jax.ShapeDtypeStruct((ne, ns, cap, D), F32),
        scratch_shapes=[pltpu.VMEM((D, de), BF16), pltpu.VMEM((D, de), BF16), pltpu.VMEM((de, D), BF16)],
        compiler_params=_cp(("arbitrary", "arbitrary")),
    )(xg, wg, wu, wd)


def _scatter_kernel(cap, idx_ref, gate_ref, y_ref, o_ref):
    s = pl.program_id(0)
    e = pl.program_id(1)
    base = (s * N_EXPERTS + e) * cap

    @pl.when(e == 0)
    def _():
        o_ref[...] = jnp.zeros(o_ref.shape, F32)

    def body(j, _):
        row = idx_ref[base + j]
        gt = gate_ref[base + j]
        o_ref[0, pl.ds(row, 1), :] = o_ref[0, pl.ds(row, 1), :] + gt * y_ref[0, 0, pl.ds(j, 1), :]
        return 0

    lax.fori_loop(0, cap, body, 0, unroll=8)


def _moe_scatter(yo, idx_flat, gate_flat, sl):
    ne, ns, cap, _ = yo.shape
    return pl.pallas_call(
        functools.partial(_scatter_kernel, cap),
        grid_spec=pltpu.PrefetchScalarGridSpec(
            num_scalar_prefetch=2, grid=(ns, ne),
            in_specs=[pl.BlockSpec((1, 1, cap, D), lambda s, e, i, g: (e, s, 0, 0))],
            out_specs=pl.BlockSpec((1, sl, D), lambda s, e, i, g: (s, 0, 0))),
        out_shape=jax.ShapeDtypeStruct((ns, sl, D), F32),
        compiler_params=_cp(("arbitrary", "arbitrary")),
    )(idx_flat, gate_flat, yo)


def _cumsum_lanes(m, tri):
    wdt = tri.shape[0]
    carry = jnp.zeros((m.shape[0], 1), F32)
    outs = []
    for b0 in range(0, m.shape[1], wdt):
        c = _dg(m[:, b0:b0 + wdt].astype(BF16), tri, _NN) + carry
        outs.append(c)
        carry = c[:, wdt - 1:wdt]
    return outs[0] if len(outs) == 1 else jnp.concatenate(outs, axis=1)


def _route_kernel(cap, p_ref, tri_ref, idx_ref, gate_ref, cs_ref):
    x = p_ref[0]
    ne, n = x.shape
    bits = lax.bitcast_convert_type(x, jnp.int32)
    cur = jnp.zeros((ne, 1), jnp.int32)
    for b in range(30, -1, -1):
        t = cur | (1 << b)
        cnt = jnp.sum(jnp.where(bits >= t, 1.0, 0.0), axis=1, keepdims=True)
        cur = jnp.where(cnt >= cap, t, cur)
    gt = bits > cur
    need = cap - jnp.sum(jnp.where(gt, 1.0, 0.0), axis=1, keepdims=True)
    tri = tri_ref[...]
    eqf = jnp.where(bits == cur, 1.0, 0.0)
    rank_eq = _cumsum_lanes(eqf, tri)
    sel = jnp.where(gt, 1.0, jnp.where(rank_eq <= need, eqf, 0.0))
    cs_ref[...] = sel * _cumsum_lanes(sel, tri)

    tpos = lax.broadcasted_iota(jnp.int32, (1, n), 1)
    thi = (tpos >> 6).astype(F32)
    tlo = (tpos & 63).astype(F32)
    jcol = (lax.broadcasted_iota(jnp.int32, (cap, 1), 0) + 1).astype(F32)
    piece = min(n, 1024)

    def body(e, _):
        csr = cs_ref[pl.ds(e, 1), :]
        g = p_ref[0, pl.ds(e, 1), :]
        gh = g.astype(BF16).astype(F32)
        r1 = g - gh
        gm = r1.astype(BF16).astype(F32)
        gl = r1 - gm
        lmat = jnp.concatenate([thi, tlo, gh, gm, gl, jnp.zeros((3, n), F32)], axis=0).astype(BF16)
        acc = jnp.zeros((8, cap), F32)
        for c0 in range(0, n, piece):
            onehot = jnp.where(csr[:, c0:c0 + piece] == jcol, 1.0, 0.0).astype(BF16)
            acc = acc + _dg(lmat[:, c0:c0 + piece], onehot, _NT)
        idx_ref[0, pl.ds(e, 1), :] = (acc[0:1] * 64.0 + acc[1:2]).astype(jnp.int32)
        gate_ref[0, pl.ds(e, 1), :] = acc[2:3] + (acc[3:4] + acc[4:5])
        return 0

    lax.fori_loop(0, ne, body, 0)


def _route_call(probs, cap):
    nreq, ne, n = probs.shape
    assert n <= 4096
    wdt = min(n, 256)
    tri = jnp.asarray(np.triu(np.ones((wdt, wdt), np.float32)), BF16)
    return pl.pallas_call(
        functools.partial(_route_kernel, cap),
        grid=(nreq,),
        in_specs=[pl.BlockSpec((1, ne, n), lambda q: (q, 0, 0)), pl.BlockSpec((wdt, wdt), lambda q: (0, 0))],
        out_specs=[pl.BlockSpec((1, ne, cap), lambda q: (q, 0, 0)), pl.BlockSpec((1, ne, cap), lambda q: (q, 0, 0))],
        out_shape=[jax.ShapeDtypeStruct((nreq, ne, cap), jnp.int32), jax.ShapeDtypeStruct((nreq, ne, cap), F32)],
        scratch_shapes=[pltpu.VMEM((ne, n), F32)],
        compiler_params=_cp(("arbitrary",)),
    )(probs, tri)


def _route(probs_t, p_b, p_l):
    ns, ne, sl = probs_t.shape
    cap_p = 2 * p_l // ne
    pp = jnp.swapaxes(probs_t[0].reshape(ne, p_b, p_l), 0, 1)
    ip, gp = _route_call(pp, cap_p)
    ip = ip + (jnp.arange(p_b, dtype=jnp.int32) * p_l)[:, None, None]
    ip = jnp.swapaxes(ip, 0, 1).reshape(1, ne, p_b * cap_p)
    gp = jnp.swapaxes(gp, 0, 1).reshape(1, ne, p_b * cap_p)
    cap = 2 * sl // ne
    is_, gs = _route_call(probs_t[1:], cap)
    idx = jnp.concatenate([ip, is_], axis=0)
    gates = jnp.concatenate([gp, gs], axis=0)
    return idx.reshape(-1), gates.reshape(-1), cap


def _final_kernel(x_ref, moe_ref, m_ref, g_ref, o_ref):
    x = x_ref[0] + m_ref[0, 5:6, :] * moe_ref[0]
    o_ref[0] = (x * lax.rsqrt(jnp.mean(x * x, axis=-1, keepdims=True) + NORM_EPS)) * g_ref[...]


def _final(x, moe, mods, g, slab0, nb, tl=512):
    ns, sl, _ = x.shape
    tok = pl.BlockSpec((1, tl, D), lambda s, t: (s + slab0, t, 0))
    return pl.pallas_call(
        _final_kernel,
        grid=(nb, sl // tl),
        in_specs=[tok, tok, pl.BlockSpec((1, 6, D), lambda s, t: (s + slab0, 0, 0)),
                  pl.BlockSpec(g.shape, lambda s, t: (0, 0))],
        out_specs=pl.BlockSpec((1, tl, D), lambda s, t: (s, t, 0)),
        out_shape=jax.ShapeDtypeStruct((nb, sl, D), F32),
        compiler_params=_cp(("arbitrary", "arbitrary")),
    )(x, moe, mods, g)


def _pad_heads(x, heads, width, padded, axis=-1):
    axis = axis % x.ndim
    shp = x.shape
    x = x.reshape(shp[:axis] + (heads, width) + shp[axis + 1:])
    pad = [(0, 0)] * x.ndim
    pad[axis + 1] = (0, padded - width)
    return jnp.pad(x, pad).reshape(shp[:axis] + (heads * padded,) + shp[axis + 1:])


def _gla_pad_cols(w):
    kw, vw = GLA_H * GLA_DK, D_GLA
    q, k, v, g, gd = (w[..., 0:kw], w[..., kw:2 * kw], w[..., 2 * kw:2 * kw + vw],
                      w[..., 2 * kw + vw:2 * kw + 2 * vw], w[..., 2 * kw + 2 * vw:])
    return jnp.concatenate([_pad_heads(q, GLA_H, GLA_DK, GLA_KP), _pad_heads(k, GLA_H, GLA_DK, GLA_KP),
                            _pad_heads(v, GLA_H, GLA_DV, GLA_VP), _pad_heads(g, GLA_H, GLA_DV, GLA_VP),
                            _pad_heads(gd, 1, 2 * GLA_R, 128)], axis=-1)


def _tri_consts():
    t = np.arange(CH)
    lower = (t[None, :] <= t[:, None]).astype(np.float32)
    return jnp.asarray(lower, BF16), jnp.asarray(lower.T, BF16)


def _layer_params(l, a):
    p = {}
    w_in = a['w_in'][l]
    p['w_lru'] = w_in[:, :2 * D_LRU].astype(BF16)
    p['w_rw'] = w_in[:, 2 * D_LRU:2 * D_LRU + RW_COLS].astype(BF16)
    p['w_gla'] = _gla_pad_cols(w_in[:, 2 * D_LRU + RW_COLS:]).astype(BF16)
    blk = jnp.asarray(np.kron(np.eye(LRU_BLOCKS, dtype=np.float32), np.ones((LRU_BW, LRU_BW), np.float32)))

    def dense(wb):
        rows = jnp.concatenate([jnp.tile(wb[n], (1, LRU_BLOCKS)) for n in range(LRU_BLOCKS)], axis=0)
        return rows * blk

    p['lru_wblk'] = jnp.concatenate([dense(a['lru_wa'][l, 0]), dense(a['lru_wx'][l, 0]),
                                     dense(a['lru_wa'][l, 1]), dense(a['lru_wx'][l, 1])], axis=1).astype(BF16)
    p['lru_bias'] = jnp.concatenate([a['lru_ba'][l, 0], a['lru_bx'][l, 0], a['lru_ba'][l, 1], a['lru_bx'][l, 1]])[None, :]
    p['lru_cw'] = a['lru_conv_w'][l]
    p['lru_cb'] = a['lru_conv_b'][l][None, :]
    p['lru_lam'] = a['lru_lambda'][l]
    lower, upper = _tri_consts()
    hones = jnp.asarray(np.tile(np.kron(np.eye(2, dtype=np.float32), np.ones((RW_HD, RW_HD), np.float32)), (3, 1)), BF16)

    def dirpad(w, d):
        z = jnp.zeros_like(w)
        return jnp.concatenate([w, z] if d == 0 else [z, w], axis=0)

    p['rw'] = {
        'mu': a['rw_mu'][l],
        'w0': [a['rw_w0'][l, d][None, :] for d in range(2)],
        'w2': [dirpad(a['rw_w2'][l, d], d) for d in range(2)],
        'a0': [a['rw_a0'][l, d][None, :] for d in range(2)],
        'a2': [dirpad(a['rw_a2'][l, d], d) for d in range(2)],
        'g2': a['rw_g2'][l],
        'kkw': a['rw_kk'][l][None, :],
        'ka': a['rw_ka'][l][None, :],
        'rk': a['rw_rk'][l].reshape(2, D_RW),
        'lnw': a['rw_ln_w'][l][None, :],
        'lnb': a['rw_ln_b'][l][None, :],
        'hones': hones,
        'tri': [jnp.kron(jnp.eye(RW_NC, dtype=BF16), lower), jnp.kron(jnp.eye(RW_NC, dtype=BF16), upper)],
    }

    def gpad(w, d):
        wp = _pad_heads(w, GLA_H, GLA_DK, GLA_KP)
        out = jnp.zeros((128, GLA_QW), F32)
        return lax.dynamic_update_slice(out, wp, (d * GLA_R, 0))

    p['gla'] = {
        'wg2': [gpad(a['gla_wg2'][l, d], d) for d in range(2)],
        'bg': [_pad_heads(a['gla_bg'][l, d][None, :], GLA_H, GLA_DK, GLA_KP) for d in range(2)],
        'nw': _pad_heads(a['gla_norm_w'][l].reshape(1, D_GLA), GLA_H, GLA_DV, GLA_VP),
        'tri': [jnp.kron(jnp.eye(GLA_NC, dtype=BF16), lower), jnp.kron(jnp.eye(GLA_NC, dtype=BF16), upper)],
        'tri_col': [jnp.kron(jnp.eye(GLA_NC_COL, dtype=BF16), lower), jnp.kron(jnp.eye(GLA_NC_COL, dtype=BF16), upper)],
    }
    w_out = a['w_out'][l]
    p['wo1'] = w_out[:D_LRU].astype(BF16)
    p['wo2'] = w_out[D_LRU:D_LRU + D_RW].astype(BF16)
    p['wo3'] = _pad_heads(w_out[D_LRU + D_RW:], GLA_H, GLA_DV, GLA_VP, axis=0).astype(BF16)
    p['wrt'] = a['w_router'][l].T
    p['norm1'] = a['norm1_g'][l][None, :]
    p['norm2'] = a['norm2_g'][l][None, :]
    return p


def _gla_state_in(sg):
    st = jnp.swapaxes(sg, -1, -2)
    st = jnp.pad(st, ((0, 0), (0, 0), (0, GLA_VP - GLA_DV), (0, GLA_KP - GLA_DK)))
    n = st.shape[0]
    return st.reshape(n, GLA_NP, 2, GLA_VP, GLA_KP).transpose(0, 1, 3, 2, 4).reshape(n, GLA_NP, GLA_VP, 2 * GLA_KP)


def _gla_state_out(sp):
    n = sp.shape[0]
    st = sp.reshape(n, GLA_NP, GLA_VP, 2, GLA_KP).transpose(0, 1, 3, 2, 4).reshape(n, GLA_H, GLA_VP, GLA_KP)
    return jnp.swapaxes(st[:, :, :GLA_DV, :GLA_DK], -1, -2)


def _forward(a):
    x_prompt, x_sample = a['x_prompt'], a['x_sample']
    p_b, p_l, _ = x_prompt.shape
    s_b, sl, _ = x_sample.shape
    assert p_b * p_l == sl
    ns = s_b + 1
    nreq = p_b + s_b

    cmat = jnp.zeros((16, D), F32).at[:s_b].set(a['c']).at[s_b].set(a['c_ctx'])
    mods_all = _mods(cmat, a['w_ada'], a['b_ada'])
    order = np.array([s_b] + list(range(s_b)))
    mods = mods_all[:, order].reshape(DEPTH, ns, 6, D)

    x = jnp.concatenate([x_prompt.reshape(1, sl, D), x_sample], axis=0)
    moe = None
    st_lru, st_rw, st_gla = [], [], []
    for l in range(DEPTH):
        p = _layer_params(l, a)
        x, lru_in, rw_cols, gla_cols = _pre(x, moe, mods[l - 1] if l else None, mods[l], p['norm1'],
                                            p['w_lru'], p['w_rw'], p['w_gla'])
        h0 = jnp.concatenate([jnp.zeros((1, 2, D_LRU), F32), a['state_lru'][:, l]], axis=0)
        o_lru, hfin = _lru(lru_in, p['lru_cw'], p['lru_cb'], p['lru_wblk'], p['lru_bias'], p['lru_lam'], h0, p_l, p_b)
        st_lru.append(jnp.swapaxes(hfin[0], 0, 1))
        zs = jnp.zeros((1, RW_H, RW_HD, RW_HD), F32)
        s0f = _rw_pack(jnp.concatenate([zs, a['state_rwkv'][:, l, 0]], axis=0))
        s0b = _rw_pack(jnp.concatenate([zs, a['state_rwkv'][:, l, 1]], axis=0))
        of, sf = _rwkv_dir(False, rw_cols, p['rw'], s0f, None, p_l, nreq)
        o_rw, sb = _rwkv_dir(True, rw_cols, p['rw'], s0b, of, p_l, nreq)
        st_rw.append(jnp.stack([_rw_unpack(sf[:p_b]), _rw_unpack(sb[:p_b])], axis=1))
        zg = jnp.zeros((p_b, GLA_NP, GLA_VP, 2 * GLA_KP), F32)
        gp_f, gsf = _gla_dir(False, False, gla_cols, p['gla'], zg, None, None, p_l // (GLA_NC * CH), 0, 1)
        gs_f, _ = _gla_dir(False, True, gla_cols, p['gla'], _gla_state_in(a['state_gla'][:, l, 0]), None, gp_f,
                           sl // (GLA_NC_COL * CH), 1, s_b)
        gp, gsb = _gla_dir(True, False, gla_cols, p['gla'], zg, gs_f, None, p_l // (GLA_NC * CH), 0, 1)
        o_gla, _ = _gla_dir(True, True, gla_cols, p['gla'], _gla_state_in(a['state_gla'][:, l, 1]), gs_f, gp,
                            sl // (GLA_NC_COL * CH), 1, s_b)
        st_gla.append(jnp.stack([_gla_state_out(gsf), _gla_state_out(gsb)], axis=1))
        x, h2, probs_t = _post(x, o_lru, o_rw, o_gla, mods[l], p['norm2'], p['wo1'], p['wo2'], p['wo3'], p['wrt'])
        idx, gates, cap = _route(probs_t, p_b, p_l)
        xg = _moe_gather(h2, idx, cap)
        yo = _moe_experts(xg, a['w_gate'], a['w_up'], a['w_down'], l)
        moe = _moe_scatter(yo, idx, gates, sl)
    fg = a['final_norm_g'][None, :]
    y_prompt = _final(x, moe, mods[DEPTH - 1], fg, 0, 1).reshape(p_b, p_l, D)
    y_sample = _final(x, moe, mods[DEPTH - 1], fg, 1, s_b)
    return (y_prompt, y_sample, jnp.stack(st_lru, axis=1), jnp.stack(st_rw, axis=1), jnp.stack(st_gla, axis=1))


def kernel(x_prompt, x_sample, c, state_lru, state_rwkv, state_gla, c_ctx, norm1_g, norm2_g, w_ada, b_ada, w_in, lru_conv_w, lru_conv_b, lru_wa, lru_ba, lru_wx, lru_bx, lru_lambda, rw_mu, rw_w0, rw_w2, rw_a0, rw_a2, rw_g2, rw_kk, rw_ka, rw_rk, rw_ln_w, rw_ln_b, gla_wg2, gla_bg, gla_norm_w, w_out, w_router, w_gate, w_up, w_down, final_norm_g):
    return _forward(dict(locals()))
```
